```python
import jax, jax.numpy as jnp
from jax import lax
import numpy as np

D_MODEL = 1024
BATCH = 2
SEQ = 8192
DEPTH = 2

GRID_W = 64
NA_HEADS = 16
NA_HEAD_DIM = D_MODEL // NA_HEADS
WIN_H = 8
WIN_W = 16
GQA_Q_HEADS = 16
GQA_KV_HEADS = 4
GQA_HEAD_DIM = D_MODEL // GQA_Q_HEADS
ROPE_THETA = 10000.0
Q_BLOCK = 128
D_FF = D_MODEL * 7 // 2
N_EXPERTS = 8
TOP_K = 2
EPS = 1e-6
N_EVEN = (DEPTH + 1) // 2
N_ODD = DEPTH // 2

kernel_name = "hybrid_natten_axial_gqa_moe_encoder"


def rmsnorm(x, g):
    xf = x.astype(jnp.float32)
    y = xf * lax.rsqrt(jnp.mean(xf * xf, axis=-1, keepdims=True) + EPS)
    return (y * g.astype(jnp.float32)).astype(x.dtype)


def rope_1d(x, pos):
    half = x.shape[-1] // 2
    freqs = ROPE_THETA ** (-jnp.arange(half, dtype=jnp.float32) / half)
    ang = pos.astype(jnp.float32)[:, None] * freqs[None, :]
    cos = jnp.cos(ang)[:, None, :]
    sin = jnp.sin(ang)[:, None, :]
    xf = x.astype(jnp.float32)
    x1, x2 = xf[..., :half], xf[..., half:]
    out = jnp.concatenate([x1 * cos - x2 * sin, x2 * cos + x1 * sin], axis=-1)
    return out.astype(x.dtype)


def axial_rope(x, row_pos, col_pos):
    half = x.shape[-1] // 2
    return jnp.concatenate([rope_1d(x[..., :half], row_pos),
                            rope_1d(x[..., half:], col_pos)], axis=-1)


def swiglu(h, w_gate, w_up, w_down):
    return (jax.nn.silu(h @ w_gate) * (h @ w_up)) @ w_down


def neighborhood_attention(h, w_qkv, rpb, w_o):
    B, S, D = h.shape
    rows = S // GRID_W
    kh = min(WIN_H, rows)
    qkv = (h @ w_qkv).reshape(B, rows, GRID_W, 3, NA_HEADS, NA_HEAD_DIM)
    q = qkv[..., 0, :, :] * (NA_HEAD_DIM ** -0.5)
    k = qkv[..., 1, :, :]
    v = qkv[..., 2, :, :]
    cols = jnp.arange(GRID_W)
    col_start = jnp.clip(cols - WIN_W // 2, 0, GRID_W - WIN_W)
    col_idx = col_start[:, None] + jnp.arange(WIN_W)[None, :]
    col_off = col_idx - cols[:, None] + (WIN_W - 1)

    def row_step(args):
        q_r, r = args
        rs = jnp.clip(r - kh // 2, 0, rows - kh)
        k_band = lax.dynamic_slice_in_dim(k, rs, kh, axis=1)
        v_band = lax.dynamic_slice_in_dim(v, rs, kh, axis=1)
        k_nb = k_band[:, :, col_idx]
        v_nb = v_band[:, :, col_idx]
        row_off = rs + jnp.arange(kh) - r + (WIN_H - 1)
        bias = rpb[:, row_off[None, :, None], col_off[:, None, :]]
        s = jnp.einsum('bqhd,biqjhd->bhqij', q_r, k_nb).astype(jnp.float32)
        s = s + bias.astype(jnp.float32)[None]
        p = jax.nn.softmax(s.reshape(B, NA_HEADS, GRID_W, kh * WIN_W), axis=-1)
        p = p.reshape(B, NA_HEADS, GRID_W, kh, WIN_W).astype(v.dtype)
        return jnp.einsum('bhqij,biqjhd->bqhd', p, v_nb)

    o = lax.map(row_step, (q.transpose(1, 0, 2, 3, 4), jnp.arange(rows)))
    o = o.transpose(1, 0, 2, 3, 4).reshape(B, S, NA_HEADS * NA_HEAD_DIM)
    return o @ w_o


def axial_gqa(h, w_qkv, q_g, k_g, w_o):
    B, S, D = h.shape
    nq = GQA_Q_HEADS * GQA_HEAD_DIM
    nkv = GQA_KV_HEADS * GQA_HEAD_DIM
    groups = GQA_Q_HEADS // GQA_KV_HEADS
    qkv = h @ w_qkv
    q = qkv[..., :nq].reshape(B, S, GQA_Q_HEADS, GQA_HEAD_DIM)
    k = qkv[..., nq:nq + nkv].reshape(B, S, GQA_KV_HEADS, GQA_HEAD_DIM)
    v = qkv[..., nq + nkv:].reshape(B, S, GQA_KV_HEADS, GQA_HEAD_DIM)
    q = rmsnorm(q, q_g)
    k = rmsnorm(k, k_g)
    t = jnp.arange(S)
    row_pos = t // GRID_W
    col_pos = t % GRID_W
    q = axial_rope(q, row_pos, col_pos) * (GQA_HEAD_DIM ** -0.5)
    k = axial_rope(k, row_pos, col_pos)
    nb = S // Q_BLOCK
    qb = q.reshape(B, nb, Q_BLOCK, GQA_KV_HEADS, groups, GQA_HEAD_DIM).transpose(1, 0, 2, 3, 4, 5)

    def block(q_blk):
        s = jnp.einsum('bqkgd,bskd->bkgqs', q_blk, k).astype(jnp.float32)
        p = jax.nn.softmax(s, axis=-1).astype(v.dtype)
        return jnp.einsum('bkgqs,bskd->bqkgd', p, v)

    o = lax.map(block, qb)
    o = o.transpose(1, 0, 2, 3, 4, 5).reshape(B, S, nq)
    return o @ w_o


def moe_swiglu(h, w_router, w_gate, w_up, w_down):
    B, S, D = h.shape
    ht = h.reshape(B * S, D)
    logits = (ht @ w_router).astype(jnp.float32)
    top_vals, top_idx = lax.top_k(logits, TOP_K)
    gates = jax.nn.softmax(top_vals, axis=-1)
    gate_full = jnp.sum(jax.nn.one_hot(top_idx, N_EXPERTS, dtype=jnp.float32) * gates[..., None], axis=1)
    y = jnp.zeros((B * S, D), jnp.float32)
    for e in range(N_EXPERTS):
        y = y + gate_full[:, e:e + 1] * swiglu(ht, w_gate[e], w_up[e], w_down[e]).astype(jnp.float32)
    return y.astype(h.dtype).reshape(B, S, D)


def setup_inputs(seed: int = 0) -> dict:
    key = jax.random.key(seed)
    ks = jax.random.split(key, 16)
    f32 = jnp.float32
    D = D_MODEL
    nq = GQA_Q_HEADS * GQA_HEAD_DIM
    nkv = GQA_KV_HEADS * GQA_HEAD_DIM
    def w(k, shape, fan_in):
        return jax.random.normal(k, shape, f32) * (fan_in ** -0.5)
    return {
        "x": jax.random.normal(ks[0], (BATCH, SEQ, D), f32),
        "norm_g": 1.0 + 0.05 * jax.random.normal(ks[1], (DEPTH, 4, D), f32),
        "na_w_qkv": w(ks[2], (N_EVEN, D, 3 * NA_HEADS * NA_HEAD_DIM), D),
        "na_rpb": 0.02 * jax.random.normal(ks[3], (N_EVEN, NA_HEADS, 2 * WIN_H - 1, 2 * WIN_W - 1), f32),
        "na_w_o": w(ks[4], (N_EVEN, NA_HEADS * NA_HEAD_DIM, D), NA_HEADS * NA_HEAD_DIM),
        "gqa_w_qkv": w(ks[5], (N_ODD, D, nq + 2 * nkv), D),
        "gqa_q_norm": 1.0 + 0.05 * jax.random.normal(ks[6], (N_ODD, GQA_HEAD_DIM), f32),
        "gqa_k_norm": 1.0 + 0.05 * jax.random.normal(ks[7], (N_ODD, GQA_HEAD_DIM), f32),
        "gqa_w_o": w(ks[8], (N_ODD, nq, D), nq),
        "ffn_w_gate": w(ks[9], (N_EVEN, D, D_FF), D),
        "ffn_w_up": w(ks[10], (N_EVEN, D, D_FF), D),
        "ffn_w_down": w(ks[11], (N_EVEN, D_FF, D), D_FF),
        "moe_w_router": w(ks[12], (N_ODD, D, N_EXPERTS), D),
        "moe_w_gate": w(ks[13], (N_ODD, N_EXPERTS, D, D_FF), D),
        "moe_w_up": w(ks[14], (N_ODD, N_EXPERTS, D, D_FF), D),
        "moe_w_down": w(ks[15], (N_ODD, N_EXPERTS, D_FF, D), D_FF),
    }


def reference(x, norm_g, na_w_qkv, na_rpb, na_w_o, gqa_w_qkv, gqa_q_norm, gqa_k_norm, gqa_w_o,
              ffn_w_gate, ffn_w_up, ffn_w_down, moe_w_router, moe_w_gate, moe_w_up, moe_w_down):
    h = x
    for i in range(DEPTH):
        j = i // 2
        a = rmsnorm(h, norm_g[i, 0])
        if i % 2 == 0:
            a = neighborhood_attention(a, na_w_qkv[j], na_rpb[j], na_w_o[j])
        else:
            a = axial_gqa(a, gqa_w_qkv[j], gqa_q_norm[j], gqa_k_norm[j], gqa_w_o[j])
        h = h + rmsnorm(a, norm_g[i, 1])
        f = rmsnorm(h, norm_g[i, 2])
        if i % 2 == 0:
            f = swiglu(f, ffn_w_gate[j], ffn_w_up[j], ffn_w_down[j])
        else:
            f = moe_swiglu(f, moe_w_router[j], moe_w_gate[j], moe_w_up[j], moe_w_down[j])
        h = h + rmsnorm(f, norm_g[i, 3])
    return h
```

```python
import functools

import jax
import jax.numpy as jnp
import numpy as np
from jax import lax
from jax.experimental import pallas as pl
from jax.experimental.pallas import tpu as pltpu

F32 = jnp.float32
BF16 = jnp.bfloat16

D_MODEL = 1024
BATCH = 2
SEQ = 8192
TOKENS = BATCH * SEQ
GRID_W = 64
GRID_H = SEQ // GRID_W
NA_HEADS = 16
HEAD_DIM = 64
WIN_H = 8
WIN_W = 16
GQA_Q_HEADS = 16
GQA_KV_HEADS = 4
GQA_GROUPS = GQA_Q_HEADS // GQA_KV_HEADS
ROPE_THETA = 10000.0
D_FF = 3584
N_EXPERTS = 8
EPS = 1e-6
LANES = 128
NEG_BIG = -1e30

VMEM_LIMIT = 56 * 1024 * 1024

NA_QROWS = 4
NA_QTOK = NA_QROWS * GRID_W
NA_BAND_BLOCKS = 3
NA_KTOK = NA_BAND_BLOCKS * NA_QTOK
NA_ROW_BLOCKS = GRID_H // NA_QROWS
NA_HEADS_PER_STEP = 4

FA_TQ = 256
FA_TK = 512


def _rms(x, g):
    ms = jnp.mean(x * x, axis=-1, keepdims=True)
    return x * lax.rsqrt(ms + EPS) * g


def _dot(a, b):
    return jnp.dot(a, b, preferred_element_type=F32)


def _dot_nt(a, b):
    return lax.dot_general(a, b, (((1,), (1,)), ((), ())), preferred_element_type=F32)


def _norm_matmul_kernel(x_ref, g_ref, w_ref, o_ref, *, n_chunk):
    xn = _rms(x_ref[...], g_ref[...]).astype(BF16)
    n = o_ref.shape[1]
    for n0 in range(0, n, n_chunk):
        o_ref[:, n0:n0 + n_chunk] = _dot(xn, w_ref[:, n0:n0 + n_chunk]).astype(o_ref.dtype)


def norm_matmul(x, g, w, tm=512, n_chunk=512):
    t, d = x.shape
    n = w.shape[1]
    return pl.pallas_call(
        functools.partial(_norm_matmul_kernel, n_chunk=n_chunk),
        grid=(t // tm,),
        in_specs=[
            pl.BlockSpec((tm, d), lambda i: (i, 0)),
            pl.BlockSpec((1, d), lambda i: (0, 0)),
            pl.BlockSpec((d, n), lambda i: (0, 0)),
        ],
        out_specs=pl.BlockSpec((tm, n), lambda i: (i, 0)),
        out_shape=jax.ShapeDtypeStruct((t, n), BF16),
        compiler_params=pltpu.CompilerParams(
            dimension_semantics=("parallel",), vmem_limit_bytes=VMEM_LIMIT),
        name="norm_matmul",
    )(x, g.reshape(1, d), w)


def _proj_norm_res_kernel(a_ref, w_ref, g_ref, h_ref, o_ref):
    y = _dot(a_ref[...], w_ref[...])
    o_ref[...] = h_ref[...] + _rms(y, g_ref[...])


def proj_norm_res(a, w, g, h, tm=512):
    t, k = a.shape
    d = w.shape[1]
    return pl.pallas_call(
        _proj_norm_res_kernel,
        grid=(t // tm,),
        in_specs=[
            pl.BlockSpec((tm, k), lambda i: (i, 0)),
            pl.BlockSpec((k, d), lambda i: (0, 0)),
            pl.BlockSpec((1, d), lambda i: (0, 0)),
            pl.BlockSpec((tm, d), lambda i: (i, 0)),
        ],
        out_specs=pl.BlockSpec((tm, d), lambda i: (i, 0)),
        out_shape=jax.ShapeDtypeStruct((t, d), F32),
        compiler_params=pltpu.CompilerParams(
            dimension_semantics=("parallel",), vmem_limit_bytes=VMEM_LIMIT),
        name="proj_norm_res",
    )(a, w, g.reshape(1, d), h)


def _na_bias_tables(rpb):
    i = np.arange(NA_QROWS)[:, None]
    m = np.arange(NA_BAND_BLOCKS * NA_QROWS)[None, :]
    ridx, rvalid = [], []
    for rb in (0, 1, NA_ROW_BLOCKS - 1):
        band0 = NA_QROWS * min(max(rb - 1, 0), NA_ROW_BLOCKS - NA_BAND_BLOCKS)
        r = NA_QROWS * rb + i
        kr = band0 + m
        rs = np.clip(r - WIN_H // 2, 0, GRID_H - WIN_H)
        rvalid.append((kr >= rs) & (kr < rs + WIN_H))
        ridx.append(np.clip(kr - r + WIN_H - 1, 0, 2 * WIN_H - 2))
    ridx = np.stack(ridx)
    rvalid = np.stack(rvalid)
    c = np.arange(GRID_W)[:, None]
    kc = np.arange(GRID_W)[None, :]
    cs = np.clip(c - WIN_W // 2, 0, GRID_W - WIN_W)
    cvalid = (kc >= cs) & (kc < cs + WIN_W)
    cidx = np.clip(kc - c + WIN_W - 1, 0, 2 * WIN_W - 2)
    rr = ridx[:, :, None, :, None]
    cc = cidx[None, None, :, None, :]
    valid = rvalid[:, :, None, :, None] & cvalid[None, None, :, None, :]
    tab = rpb[:, rr, cc]
    tab = jnp.where(valid[None], tab, NEG_BIG)
    return tab.transpose(1, 0, 2, 3, 4, 5).reshape(3, NA_HEADS, NA_QTOK, NA_KTOK)


def _na_kernel(q_ref, k0_ref, k1_ref, k2_ref, v0_ref, v1_ref, v2_ref, bias_ref, o_ref):
    lane = lax.broadcasted_iota(jnp.int32, (1, LANES), 1)
    low = lane < HEAD_DIM
    for pair in range(NA_HEADS_PER_STEP // 2):
        sl = slice(pair * LANES, (pair + 1) * LANES)
        q = q_ref[:, sl]
        k = jnp.concatenate([k0_ref[:, sl], k1_ref[:, sl], k2_ref[:, sl]], axis=0)
        v = jnp.concatenate([v0_ref[:, sl], v1_ref[:, sl], v2_ref[:, sl]], axis=0)
        outs = []
        for half in range(2):
            keep = low if half == 0 else jnp.logical_not(low)
            qh = jnp.where(keep, q, jnp.zeros_like(q))
            s = _dot_nt(qh, k) + bias_ref[0, 2 * pair + half]
            mx = jnp.max(s, axis=-1, keepdims=True)
            p = jnp.exp(s - mx)
            l = jnp.sum(p, axis=-1, keepdims=True)
            outs.append(_dot(p.astype(BF16), v) / l)
        o_ref[:, sl] = jnp.where(low, outs[0], outs[1]).astype(o_ref.dtype)


def neighborhood_attention(qkv, bias_tab):
    hw = NA_HEADS_PER_STEP * HEAD_DIM
    nh = D_MODEL // hw
    blocks_per_seq = SEQ // NA_QTOK

    def band(rb):
        return jnp.clip(rb - 1, 0, NA_ROW_BLOCKS - NA_BAND_BLOCKS)

    def case(rb):
        return jnp.where(rb == 0, 0, jnp.where(rb == NA_ROW_BLOCKS - 1, 2, 1))

    q_spec = pl.BlockSpec((NA_QTOK, hw), lambda h, b, rb: (b * blocks_per_seq + rb, h))
    kv_specs = [
        pl.BlockSpec((NA_QTOK, hw),
                     lambda h, b, rb, j=j, part=part: (b * blocks_per_seq + band(rb) + j, part * nh + h))
        for part in (1, 2) for j in range(NA_BAND_BLOCKS)
    ]
    bias_spec = pl.BlockSpec((1, NA_HEADS_PER_STEP, NA_QTOK, NA_KTOK),
                             lambda h, b, rb: (case(rb), h, 0, 0))
    return pl.pallas_call(
        _na_kernel,
        grid=(nh, BATCH, NA_ROW_BLOCKS),
        in_specs=[q_spec] + kv_specs + [bias_spec],
        out_specs=pl.BlockSpec((NA_QTOK, hw), lambda h, b, rb: (b * blocks_per_seq + rb, h)),
        out_shape=jax.ShapeDtypeStruct((TOKENS, D_MODEL), BF16),
        compiler_params=pltpu.CompilerParams(
            dimension_semantics=("parallel", "parallel", "parallel"), vmem_limit_bytes=VMEM_LIMIT),
        name="na_attention",
    )(qkv, qkv, qkv, qkv, qkv, qkv, qkv, bias_tab)


def _ffn_kernel(h_ref, gin_ref, wg_ref, wu_ref, wd_ref, gout_ref, o_ref, xn_ref, acc_ref):
    j = pl.program_id(1)

    @pl.when(j == 0)
    def _():
        xn_ref[...] = _rms(h_ref[...], gin_ref[...]).astype(BF16)
        acc_ref[...] = jnp.zeros_like(acc_ref)

    xn = xn_ref[...]
    gate = _dot(xn, wg_ref[...])
    up = _dot(xn, wu_ref[...])
    act = (gate * jax.nn.sigmoid(gate) * up).astype(BF16)
    acc_ref[...] += _dot(act, wd_ref[...])

    @pl.when(j == pl.num_programs(1) - 1)
    def _():
        o_ref[...] = h_ref[...] + _rms(acc_ref[...], gout_ref[...])


def ffn_block(h, g_in, wg, wu, wd, g_out, tm=1024, tf=512):
    t, d = h.shape
    f = wg.shape[1]
    return pl.pallas_call(
        _ffn_kernel,
        grid=(t // tm, f // tf),
        in_specs=[
            pl.BlockSpec((tm, d), lambda i, j: (i, 0)),
            pl.BlockSpec((1, d), lambda i, j: (0, 0)),
            pl.BlockSpec((d, tf), lambda i, j: (0, j)),
            pl.BlockSpec((d, tf), lambda i, j: (0, j)),
            pl.BlockSpec((tf, d), lambda i, j: (j, 0)),
            pl.BlockSpec((1, d), lambda i, j: (0, 0)),
        ],
        out_specs=pl.BlockSpec((tm, d), lambda i, j: (i, 0)),
        out_shape=jax.ShapeDtypeStruct((t, d), F32),
        scratch_shapes=[pltpu.VMEM((tm, d), BF16), pltpu.VMEM((tm, d), F32)],
        compiler_params=pltpu.CompilerParams(
            dimension_semantics=("parallel", "arbitrary"), vmem_limit_bytes=VMEM_LIMIT),
        name="ffn_swiglu",
    )(h, g_in.reshape(1, d), wg, wu, wd, g_out.reshape(1, d))


def _pad_heads(w, n_heads):
    d = w.shape[0]
    w = w.reshape(d, n_heads, HEAD_DIM)
    return jnp.pad(w, ((0, 0), (0, 0), (0, LANES - HEAD_DIM))).reshape(d, n_heads * LANES)


def _rope_tables():
    quarter = HEAD_DIM // 4
    freqs = ROPE_THETA ** (-jnp.arange(quarter, dtype=F32) / quarter)
    t = jnp.arange(SEQ)
    row_ang = (t // GRID_W).astype(F32)[:, None] * freqs[None, :]
    col_ang = (t % GRID_W).astype(F32)[:, None] * freqs[None, :]
    zeros = jnp.zeros((SEQ, LANES - HEAD_DIM), F32)
    cos = jnp.concatenate([jnp.cos(row_ang), jnp.cos(row_ang), jnp.cos(col_ang), jnp.cos(col_ang), zeros], axis=1)
    sin = jnp.concatenate([-jnp.sin(row_ang), jnp.sin(row_ang), -jnp.sin(col_ang), jnp.sin(col_ang), zeros], axis=1)
    return cos, sin


def _gqa_qkv_kernel(x_ref, g_ref, w_ref, qg_ref, kg_ref, cos_ref, sin_ref, q_ref, k_ref, v_ref):
    xn = _rms(x_ref[...], g_ref[...]).astype(BF16)
    lane = lax.broadcasted_iota(jnp.int32, (1, LANES), 1)
    first = (lane % (HEAD_DIM // 2)) < (HEAD_DIM // 4)
    cos = cos_ref[...]
    sin = sin_ref[...]
    q_scale = HEAD_DIM ** -0.5

    def norm_rope(y, gain, scale):
        ms = jnp.sum(y * y, axis=-1, keepdims=True) * (1.0 / HEAD_DIM)
        y = y * lax.rsqrt(ms + EPS) * gain
        partner = jnp.where(first, pltpu.roll(y, LANES - HEAD_DIM // 4, 1), pltpu.roll(y, HEAD_DIM // 4, 1))
        return (y * cos + partner * sin) * scale

    for hh in range(GQA_Q_HEADS):
        y = _dot(xn, w_ref[:, hh * LANES:(hh + 1) * LANES])
        q_ref[:, hh * LANES:(hh + 1) * LANES] = norm_rope(y, qg_ref[...], q_scale).astype(BF16)
    k_off = GQA_Q_HEADS * LANES
    v_off = k_off + GQA_KV_HEADS * LANES
    for hh in range(GQA_KV_HEADS):
        y = _dot(xn, w_ref[:, k_off + hh * LANES:k_off + (hh + 1) * LANES])
        k_ref[:, hh * LANES:(hh + 1) * LANES] = norm_rope(y, kg_ref[...], 1.0).astype(BF16)
        y = _dot(xn, w_ref[:, v_off + hh * LANES:v_off + (hh + 1) * LANES])
        v_ref[:, hh * LANES:(hh + 1) * LANES] = jnp.where(lane == HEAD_DIM, 1.0, y).astype(BF16)


def gqa_qkv(h, g, w_pad, q_gain, k_gain, cos, sin, tm=512):
    t, d = h.shape
    n = w_pad.shape[1]
    nq = GQA_Q_HEADS * LANES
    nkv = GQA_KV_HEADS * LANES
    seq_blocks = SEQ // tm
    pad = lambda v: jnp.pad(v, (0, LANES - HEAD_DIM)).reshape(1, LANES)
    return pl.pallas_call(
        _gqa_qkv_kernel,
        grid=(t // tm,),
        in_specs=[
            pl.BlockSpec((tm, d), lambda i: (i, 0)),
            pl.BlockSpec((1, d), lambda i: (0, 0)),
            pl.BlockSpec((d, n), lambda i: (0, 0)),
            pl.BlockSpec((1, LANES), lambda i: (0, 0)),
            pl.BlockSpec((1, LANES), lambda i: (0, 0)),
            pl.BlockSpec((tm, LANES), lambda i: (i % seq_blocks, 0)),
            pl.BlockSpec((tm, LANES), lambda i: (i % seq_blocks, 0)),
        ],
        out_specs=[
            pl.BlockSpec((tm, nq), lambda i: (i, 0)),
            pl.BlockSpec((tm, nkv), lambda i: (i, 0)),
            pl.BlockSpec((tm, nkv), lambda i: (i, 0)),
        ],
        out_shape=[
            jax.ShapeDtypeStruct((t, nq), BF16),
            jax.ShapeDtypeStruct((t, nkv), BF16),
            jax.ShapeDtypeStruct((t, nkv), BF16),
        ],
        compiler_params=pltpu.CompilerParams(
            dimension_semantics=("parallel",), vmem_limit_bytes=VMEM_LIMIT),
        name="gqa_qkv",
    )(h, g.reshape(1, d), w_pad, pad(q_gain), pad(k_gain), cos, sin)


def _flash_kernel(q_ref, k_ref, v_ref, o_ref, qs_ref, m_ref, acc_ref):
    rows = GQA_GROUPS * FA_TQ
    for g in range(GQA_GROUPS):
        qs_ref[g * FA_TQ:(g + 1) * FA_TQ, :] = q_ref[:, g * LANES:(g + 1) * LANES]
    m_ref[...] = jnp.full(m_ref.shape, -jnp.inf, F32)
    acc_ref[...] = jnp.zeros_like(acc_ref)

    def body(j, carry):
        start = pl.multiple_of(j * FA_TK, FA_TK)
        kj = k_ref[pl.ds(start, FA_TK), :]
        vj = v_ref[pl.ds(start, FA_TK), :]
        s = _dot_nt(qs_ref[...], kj)
        m_prev = m_ref[:, :1]
        m_new = jnp.maximum(m_prev, jnp.max(s, axis=-1, keepdims=True))
        p = jnp.exp(s - m_new)
        alpha = jnp.exp(m_prev - m_new)
        acc_ref[...] = alpha * acc_ref[...] + _dot(p.astype(BF16), vj)
        m_ref[...] = jnp.broadcast_to(m_new, (rows, LANES))
        return carry

    lax.fori_loop(0, SEQ // FA_TK, body, 0)
    acc = acc_ref[...]
    out = acc / acc[:, HEAD_DIM:HEAD_DIM + 1]
    for g in range(GQA_GROUPS):
        o_ref[:, g * LANES:(g + 1) * LANES] = out[g * FA_TQ:(g + 1) * FA_TQ, :].astype(o_ref.dtype)


def flash_attention(q, k, v):
    gw = GQA_GROUPS * LANES
    q_blocks = SEQ // FA_TQ
    rows = GQA_GROUPS * FA_TQ
    return pl.pallas_call(
        _flash_kernel,
        grid=(BATCH, GQA_KV_HEADS, q_blocks),
        in_specs=[
            pl.BlockSpec((FA_TQ, gw), lambda b, kh, i: (b * q_blocks + i, kh)),
            pl.BlockSpec((SEQ, LANES), lambda b, kh, i: (b, kh)),
            pl.BlockSpec((SEQ, LANES), lambda b, kh, i: (b, kh)),
        ],
        out_specs=pl.BlockSpec((FA_TQ, gw), lambda b, kh, i: (b * q_blocks + i, kh)),
        out_shape=jax.ShapeDtypeStruct((TOKENS, GQA_Q_HEADS * LANES), BF16),
        scratch_shapes=[
            pltpu.VMEM((rows, LANES), BF16),
            pltpu.VMEM((rows, LANES), F32),
            pltpu.VMEM((rows, LANES), F32),
        ],
        compiler_params=pltpu.CompilerParams(
            dimension_semantics=("parallel", "parallel", "parallel"), vmem_limit_bytes=VMEM_LIMIT),
        name="gqa_flash",
    )(q, k, v)


def _moe_kernel(h_ref, gin_ref, wr_ref, wg_ref, wu_ref, wd_ref, gout_ref, o_ref,
                xn_ref, gates_ref, ecol_ref, eacc_ref, acc_ref):
    e = pl.program_id(1)
    j = pl.program_id(2)
    last_j = pl.num_programs(2) - 1
    lane = lax.broadcasted_iota(jnp.int32, (1, LANES), 1)

    @pl.when(jnp.logical_and(e == 0, j == 0))
    def _():
        x = _rms(h_ref[...], gin_ref[...])
        xn_ref[...] = x.astype(BF16)
        logits = jnp.dot(x, wr_ref[...], preferred_element_type=F32, precision=lax.Precision.HIGHEST)
        logits = jnp.where(lane < N_EXPERTS, logits, -jnp.inf)
        m1 = jnp.max(logits, axis=-1, keepdims=True)
        i1 = jnp.min(jnp.where(logits == m1, lane, LANES), axis=-1, keepdims=True)
        rest = jnp.where(lane == i1, -jnp.inf, logits)
        m2 = jnp.max(rest, axis=-1, keepdims=True)
        i2 = jnp.min(jnp.where(rest == m2, lane, LANES), axis=-1, keepdims=True)
        e2 = jnp.exp(m2 - m1)
        denom = 1.0 + e2
        gates_ref[...] = jnp.where(lane == i1, 1.0 / denom, 0.0) + jnp.where(lane == i2, e2 / denom, 0.0)
        acc_ref[...] = jnp.zeros_like(acc_ref)

    @pl.when(j == 0)
    def _():
        col = jnp.sum(jnp.where(lane == e, gates_ref[...], 0.0), axis=-1, keepdims=True)
        ecol_ref[...] = jnp.broadcast_to(col, ecol_ref.shape)
        eacc_ref[...] = jnp.zeros_like(eacc_ref)

    xn = xn_ref[...]
    gate = _dot(xn, wg_ref[0])
    up = _dot(xn, wu_ref[0])
    act = (gate * jax.nn.sigmoid(gate) * up).astype(BF16)
    eacc_ref[...] += _dot(act, wd_ref[0])

    @pl.when(j == last_j)
    def _():
        acc_ref[...] += ecol_ref[:, :1] * eacc_ref[...]

    @pl.when(jnp.logical_and(e == pl.num_programs(1) - 1, j == last_j))
    def _():
        o_ref[...] = h_ref[...] + _rms(acc_ref[...], gout_ref[...])


def moe_block(h, g_in, w_router, wg, wu, wd, g_out, tm=1024, tf=512):
    t, d = h.shape
    n_e, _, f = wg.shape
    wr = jnp.pad(w_router, ((0, 0), (0, LANES - n_e)))
    return pl.pallas_call(
        _moe_kernel,
        grid=(t // tm, n_e, f // tf),
        in_specs=[
            pl.BlockSpec((tm, d), lambda i, e, j: (i, 0)),
            pl.BlockSpec((1, d), lambda i, e, j: (0, 0)),
            pl.BlockSpec((d, LANES), lambda i, e, j: (0, 0)),
            pl.BlockSpec((1, d, tf), lambda i, e, j: (e, 0, j)),
            pl.BlockSpec((1, d, tf), lambda i, e, j: (e, 0, j)),
            pl.BlockSpec((1, tf, d), lambda i, e, j: (e, j, 0)),
            pl.BlockSpec((1, d), lambda i, e, j: (0, 0)),
        ],
        out_specs=pl.BlockSpec((tm, d), lambda i, e, j: (i, 0)),
        out_shape=jax.ShapeDtypeStruct((t, d), F32),
        scratch_shapes=[
            pltpu.VMEM((tm, d), BF16),
            pltpu.VMEM((tm, LANES), F32),
            pltpu.VMEM((tm, LANES), F32),
            pltpu.VMEM((tm, d), F32),
            pltpu.VMEM((tm, d), F32),
        ],
        compiler_params=pltpu.CompilerParams(
            dimension_semantics=("parallel", "arbitrary", "arbitrary"), vmem_limit_bytes=VMEM_LIMIT),
        name="moe_swiglu",
    )(h, g_in.reshape(1, d), wr, wg, wu, wd, g_out.reshape(1, d))


def kernel(x, norm_g, na_w_qkv, na_rpb, na_w_o, gqa_w_qkv, gqa_q_norm, gqa_k_norm, gqa_w_o,
           ffn_w_gate, ffn_w_up, ffn_w_down, moe_w_router, moe_w_gate, moe_w_up, moe_w_down):
    assert x.shape == (BATCH, SEQ, D_MODEL)
    h = x.reshape(TOKENS, D_MODEL)

    w_qkv = na_w_qkv[0].at[:, :D_MODEL].multiply(HEAD_DIM ** -0.5).astype(BF16)
    qkv = norm_matmul(h, norm_g[0, 0], w_qkv)
    a = neighborhood_attention(qkv, _na_bias_tables(na_rpb[0]))
    h = proj_norm_res(a, na_w_o[0].astype(BF16), norm_g[0, 1], h)
    h = ffn_block(h, norm_g[0, 2], ffn_w_gate[0].astype(BF16), ffn_w_up[0].astype(BF16),
                  ffn_w_down[0].astype(BF16), norm_g[0, 3])

    nq = GQA_Q_HEADS * HEAD_DIM
    nkv = GQA_KV_HEADS * HEAD_DIM
    wq, wk, wv = gqa_w_qkv[0][:, :nq], gqa_w_qkv[0][:, nq:nq + nkv], gqa_w_qkv[0][:, nq + nkv:]
    w_pad = jnp.concatenate([_pad_heads(wq, GQA_Q_HEADS), _pad_heads(wk, GQA_KV_HEADS),
                             _pad_heads(wv, GQA_KV_HEADS)], axis=1).astype(BF16)
    cos, sin = _rope_tables()
    q, k, v = gqa_qkv(h, norm_g[1, 0], w_pad, gqa_q_norm[0], gqa_k_norm[0], cos, sin)
    o = flash_attention(q, k, v)
    wo_pad = jnp.pad(gqa_w_o[0].reshape(GQA_Q_HEADS, HEAD_DIM, D_MODEL),
                     ((0, 0), (0, LANES - HEAD_DIM), (0, 0))).reshape(GQA_Q_HEADS * LANES, D_MODEL)
    h = proj_norm_res(o, wo_pad.astype(BF16), norm_g[1, 1], h)
    h = moe_block(h, norm_g[1, 2], moe_w_router[0], moe_w_gate[0].astype(BF16), moe_w_up[0].astype(BF16),
                  moe_w_down[0].astype(BF16), norm_g[1, 3])
    return h.reshape(BATCH, SEQ, D_MODEL)
```

```python
import functools

import jax
import jax.numpy as jnp
import numpy as np
from jax import lax
from jax.experimental import pallas as pl
from jax.experimental.pallas import tpu as pltpu

F32 = jnp.float32
BF16 = jnp.bfloat16

D_MODEL = 1024
BATCH = 2
SEQ = 8192
TOKENS = BATCH * SEQ
GRID_W = 64
GRID_H = SEQ // GRID_W
NA_HEADS = 16
HEAD_DIM = 64
WIN_H = 8
WIN_W = 16
GQA_Q_HEADS = 16
GQA_KV_HEADS = 4
GQA_GROUPS = GQA_Q_HEADS // GQA_KV_HEADS
ROPE_THETA = 10000.0
D_FF = 3584
N_EXPERTS = 8
EPS = 1e-6
LANES = 128
NEG_BIG = -1e30

VMEM_LIMIT = 56 * 1024 * 1024

NA_QROWS = 4
NA_QTOK = NA_QROWS * GRID_W
NA_BAND_BLOCKS = 3
NA_KTOK = NA_BAND_BLOCKS * NA_QTOK
NA_ROW_BLOCKS = GRID_H // NA_QROWS
NA_HEADS_PER_STEP = 4

FA_TQ = 512
FA_TK = 512
FA_VROWS = 80


def _rms(x, g):
    ms = jnp.mean(x * x, axis=-1, keepdims=True)
    return x * lax.rsqrt(ms + EPS) * g


def _dot(a, b):
    return jnp.dot(a, b, preferred_element_type=F32)


def _dot_nt(a, b):
    return lax.dot_general(a, b, (((1,), (1,)), ((), ())), preferred_element_type=F32)


def _norm_matmul_kernel(x_ref, g_ref, w_ref, o_ref, *, n_chunk):
    xn = _rms(x_ref[...], g_ref[...]).astype(BF16)
    n = o_ref.shape[1]
    for n0 in range(0, n, n_chunk):
        o_ref[:, n0:n0 + n_chunk] = _dot(xn, w_ref[:, n0:n0 + n_chunk]).astype(o_ref.dtype)


def norm_matmul(x, g, w, tm=512, n_chunk=512):
    t, d = x.shape
    n = w.shape[1]
    return pl.pallas_call(
        functools.partial(_norm_matmul_kernel, n_chunk=n_chunk),
        grid=(t // tm,),
        in_specs=[
            pl.BlockSpec((tm, d), lambda i: (i, 0)),
            pl.BlockSpec((1, d), lambda i: (0, 0)),
            pl.BlockSpec((d, n), lambda i: (0, 0)),
        ],
        out_specs=pl.BlockSpec((tm, n), lambda i: (i, 0)),
        out_shape=jax.ShapeDtypeStruct((t, n), BF16),
        compiler_params=pltpu.CompilerParams(
            dimension_semantics=("parallel",), vmem_limit_bytes=VMEM_LIMIT),
        name="norm_matmul",
    )(x, g.reshape(1, d), w)


def _proj_norm_res_kernel(a_ref, w_ref, g_ref, h_ref, o_ref):
    y = _dot(a_ref[...], w_ref[...])
    o_ref[...] = h_ref[...] + _rms(y, g_ref[...])


def proj_norm_res(a, w, g, h, tm=512):
    t, k = a.shape
    d = w.shape[1]
    return pl.pallas_call(
        _proj_norm_res_kernel,
        grid=(t // tm,),
        in_specs=[
            pl.BlockSpec((tm, k), lambda i: (i, 0)),
            pl.BlockSpec((k, d), lambda i: (0, 0)),
            pl.BlockSpec((1, d), lambda i: (0, 0)),
            pl.BlockSpec((tm, d), lambda i: (i, 0)),
        ],
        out_specs=pl.BlockSpec((tm, d), lambda i: (i, 0)),
        out_shape=jax.ShapeDtypeStruct((t, d), F32),
        compiler_params=pltpu.CompilerParams(
            dimension_semantics=("parallel",), vmem_limit_bytes=VMEM_LIMIT),
        name="proj_norm_res",
    )(a, w, g.reshape(1, d), h)


def _na_bias_tables(rpb):
    i = np.arange(NA_QROWS)[:, None]
    m = np.arange(NA_BAND_BLOCKS * NA_QROWS)[None, :]
    ridx, rvalid = [], []
    for rb in (0, 1, NA_ROW_BLOCKS - 1):
        band0 = NA_QROWS * min(max(rb - 1, 0), NA_ROW_BLOCKS - NA_BAND_BLOCKS)
        r = NA_QROWS * rb + i
        kr = band0 + m
        rs = np.clip(r - WIN_H // 2, 0, GRID_H - WIN_H)
        rvalid.append((kr >= rs) & (kr < rs + WIN_H))
        ridx.append(np.clip(kr - r + WIN_H - 1, 0, 2 * WIN_H - 2))
    n_dr = 2 * WIN_H - 1
    plane = np.where(np.stack(rvalid), np.stack(ridx), n_dr).reshape(-1)
    c = np.arange(GRID_W)[:, None]
    kc = np.arange(GRID_W)[None, :]
    cs = np.clip(c - WIN_W // 2, 0, GRID_W - WIN_W)
    cvalid = (kc >= cs) & (kc < cs + WIN_W)
    left = GRID_W - WIN_W
    padded = jnp.pad(rpb, ((0, 0), (0, 0), (left, left)))
    cols = jnp.stack([padded[:, :, GRID_W - 1 - cq:2 * GRID_W - 1 - cq] for cq in range(GRID_W)], axis=2)
    cols = jnp.where(cvalid[None, None], cols, NEG_BIG)
    cols = jnp.concatenate([cols, jnp.full_like(cols[:, :1], NEG_BIG)], axis=1)
    tab = jnp.concatenate([cols[:, u:u + 1] for u in plane], axis=1)
    tab = tab.reshape(NA_HEADS, 3, NA_QROWS, NA_BAND_BLOCKS * NA_QROWS, GRID_W, GRID_W)
    return tab.transpose(1, 0, 2, 4, 3, 5).reshape(3, NA_HEADS, NA_QTOK, NA_KTOK)


def _na_kernel(q_ref, k0_ref, k1_ref, k2_ref, v0_ref, v1_ref, v2_ref, bias_ref, o_ref):
    lane = lax.broadcasted_iota(jnp.int32, (1, LANES), 1)
    low = lane < HEAD_DIM
    for pair in range(NA_HEADS_PER_STEP // 2):
        sl = slice(pair * LANES, (pair + 1) * LANES)
        q = q_ref[:, sl]
        k = jnp.concatenate([k0_ref[:, sl], k1_ref[:, sl], k2_ref[:, sl]], axis=0)
        v = jnp.concatenate([v0_ref[:, sl], v1_ref[:, sl], v2_ref[:, sl]], axis=0)
        outs = []
        for half in range(2):
            keep = low if half == 0 else jnp.logical_not(low)
            qh = jnp.where(keep, q, jnp.zeros_like(q))
            s = _dot_nt(qh, k) + bias_ref[0, 2 * pair + half]
            mx = jnp.max(s, axis=-1, keepdims=True)
            p = jnp.exp(s - mx)
            l = jnp.sum(p, axis=-1, keepdims=True)
            outs.append(_dot(p.astype(BF16), v) / l)
        o_ref[:, sl] = jnp.where(low, outs[0], outs[1]).astype(o_ref.dtype)


def neighborhood_attention(qkv, bias_tab):
    hw = NA_HEADS_PER_STEP * HEAD_DIM
    nh = D_MODEL // hw
    blocks_per_seq = SEQ // NA_QTOK

    def band(rb):
        return jnp.clip(rb - 1, 0, NA_ROW_BLOCKS - NA_BAND_BLOCKS)

    def case(rb):
        return jnp.where(rb == 0, 0, jnp.where(rb == NA_ROW_BLOCKS - 1, 2, 1))

    q_spec = pl.BlockSpec((NA_QTOK, hw), lambda h, b, rb: (b * blocks_per_seq + rb, h))
    kv_specs = [
        pl.BlockSpec((NA_QTOK, hw),
                     lambda h, b, rb, j=j, part=part: (b * blocks_per_seq + band(rb) + j, part * nh + h))
        for part in (1, 2) for j in range(NA_BAND_BLOCKS)
    ]
    bias_spec = pl.BlockSpec((1, NA_HEADS_PER_STEP, NA_QTOK, NA_KTOK),
                             lambda h, b, rb: (case(rb), h, 0, 0))
    return pl.pallas_call(
        _na_kernel,
        grid=(nh, BATCH, NA_ROW_BLOCKS),
        in_specs=[q_spec] + kv_specs + [bias_spec],
        out_specs=pl.BlockSpec((NA_QTOK, hw), lambda h, b, rb: (b * blocks_per_seq + rb, h)),
        out_shape=jax.ShapeDtypeStruct((TOKENS, D_MODEL), BF16),
        compiler_params=pltpu.CompilerParams(
            dimension_semantics=("parallel", "parallel", "parallel"), vmem_limit_bytes=VMEM_LIMIT),
        name="na_attention",
    )(qkv, qkv, qkv, qkv, qkv, qkv, qkv, bias_tab)


def _ffn_kernel(h_ref, gin_ref, wg_ref, wu_ref, wd_ref, gout_ref, o_ref, xn_ref, acc_ref):
    j = pl.program_id(1)

    @pl.when(j == 0)
    def _():
        xn_ref[...] = _rms(h_ref[...], gin_ref[...]).astype(BF16)
        acc_ref[...] = jnp.zeros_like(acc_ref)

    xn = xn_ref[...]
    gate = _dot(xn, wg_ref[...])
    up = _dot(xn, wu_ref[...])
    act = (gate * jax.nn.sigmoid(gate) * up).astype(BF16)
    acc_ref[...] += _dot(act, wd_ref[...])

    @pl.when(j == pl.num_programs(1) - 1)
    def _():
        o_ref[...] = h_ref[...] + _rms(acc_ref[...], gout_ref[...])


def ffn_block(h, g_in, wg, wu, wd, g_out, tm=1024, tf=512):
    t, d = h.shape
    f = wg.shape[1]
    return pl.pallas_call(
        _ffn_kernel,
        grid=(t // tm, f // tf),
        in_specs=[
            pl.BlockSpec((tm, d), lambda i, j: (i, 0)),
            pl.BlockSpec((1, d), lambda i, j: (0, 0)),
            pl.BlockSpec((d, tf), lambda i, j: (0, j)),
            pl.BlockSpec((d, tf), lambda i, j: (0, j)),
            pl.BlockSpec((tf, d), lambda i, j: (j, 0)),
            pl.BlockSpec((1, d), lambda i, j: (0, 0)),
        ],
        out_specs=pl.BlockSpec((tm, d), lambda i, j: (i, 0)),
        out_shape=jax.ShapeDtypeStruct((t, d), F32),
        scratch_shapes=[pltpu.VMEM((tm, d), BF16), pltpu.VMEM((tm, d), F32)],
        compiler_params=pltpu.CompilerParams(
            dimension_semantics=("parallel", "arbitrary"), vmem_limit_bytes=VMEM_LIMIT),
        name="ffn_swiglu",
    )(h, g_in.reshape(1, d), wg, wu, wd, g_out.reshape(1, d))


def _pad_heads(w, n_heads, width):
    d = w.shape[0]
    w = w.reshape(d, n_heads, HEAD_DIM)
    return jnp.pad(w, ((0, 0), (0, 0), (0, width - HEAD_DIM))).reshape(d, n_heads * width)


def _rope_tables():
    quarter = HEAD_DIM // 4
    freqs = ROPE_THETA ** (-jnp.arange(quarter, dtype=F32) / quarter)
    t = jnp.arange(SEQ)
    row_ang = (t // GRID_W).astype(F32)[:, None] * freqs[None, :]
    col_ang = (t % GRID_W).astype(F32)[:, None] * freqs[None, :]
    zeros = jnp.zeros((SEQ, LANES - HEAD_DIM), F32)
    cos = jnp.concatenate([jnp.cos(row_ang), jnp.cos(row_ang), jnp.cos(col_ang), jnp.cos(col_ang), zeros], axis=1)
    sin = jnp.concatenate([-jnp.sin(row_ang), jnp.sin(row_ang), -jnp.sin(col_ang), jnp.sin(col_ang), zeros], axis=1)
    return cos, sin


def _gqa_qkv_kernel(x_ref, g_ref, wqt_ref, wk_ref, wvt_ref, qg_ref, kg_ref, cos_ref, sin_ref, cost_ref, sint_ref,
                    qt_ref, k_ref, vt_ref):
    xn = _rms(x_ref[...], g_ref[...]).astype(BF16)
    quarter = HEAD_DIM // 4

    cost = cost_ref[...]
    sint = sint_ref[...]
    qg = qg_ref[...]
    for hh in range(GQA_Q_HEADS):
        y = _dot_nt(wqt_ref[hh * LANES:(hh + 1) * LANES, :], xn)
        ms = jnp.sum(y * y, axis=0, keepdims=True) * (1.0 / HEAD_DIM)
        y = y * lax.rsqrt(ms + EPS) * qg
        partner = jnp.concatenate(
            [y[quarter:2 * quarter], y[:quarter], y[3 * quarter:HEAD_DIM], y[2 * quarter:3 * quarter], y[HEAD_DIM:]],
            axis=0)
        qt_ref[0, hh * LANES:(hh + 1) * LANES, :] = (y * cost + partner * sint).astype(BF16)

    lane = lax.broadcasted_iota(jnp.int32, (1, LANES), 1)
    first = (lane % (2 * quarter)) < quarter
    cos = cos_ref[...]
    sin = sin_ref[...]
    row = lax.broadcasted_iota(jnp.int32, (FA_VROWS, 1), 0)
    for hh in range(GQA_KV_HEADS):
        y = _dot(xn, wk_ref[:, hh * LANES:(hh + 1) * LANES])
        ms = jnp.sum(y * y, axis=-1, keepdims=True) * (1.0 / HEAD_DIM)
        y = y * lax.rsqrt(ms + EPS) * kg_ref[...]
        partner = jnp.where(first, pltpu.roll(y, LANES - quarter, 1), pltpu.roll(y, quarter, 1))
        k_ref[:, hh * LANES:(hh + 1) * LANES] = (y * cos + partner * sin).astype(BF16)
        vt = _dot_nt(wvt_ref[hh * FA_VROWS:(hh + 1) * FA_VROWS, :], xn)
        vt_ref[0, hh * FA_VROWS:(hh + 1) * FA_VROWS, :] = jnp.where(row == HEAD_DIM, 1.0, vt).astype(BF16)


def gqa_qkv(h, g, wq, wk, wv, q_gain, k_gain):
    t, d = h.shape
    tm = FA_TK
    seq_blocks = SEQ // tm
    wqt = _pad_heads(wq, GQA_Q_HEADS, LANES).T.astype(BF16)
    wkp = _pad_heads(wk, GQA_KV_HEADS, LANES).astype(BF16)
    wvt = _pad_heads(wv, GQA_KV_HEADS, FA_VROWS).T.astype(BF16)
    cos, sin = _rope_tables()
    q_scale = (HEAD_DIM ** -0.5) * float(np.log2(np.e))
    cost, sint = (cos * q_scale).T, (sin * q_scale).T
    qg = jnp.pad(q_gain, (0, LANES - HEAD_DIM)).reshape(LANES, 1)
    kg = jnp.pad(k_gain, (0, LANES - HEAD_DIM)).reshape(1, LANES)
    const = lambda shape: pl.BlockSpec(shape, lambda i: (0,) * len(shape))
    return pl.pallas_call(
        _gqa_qkv_kernel,
        grid=(t // tm,),
        in_specs=[
            pl.BlockSpec((tm, d), lambda i: (i, 0)),
            const((1, d)),
            const(wqt.shape),
            const(wkp.shape),
            const(wvt.shape),
            const((LANES, 1)),
            const((1, LANES)),
            pl.BlockSpec((tm, LANES), lambda i: (i % seq_blocks, 0)),
            pl.BlockSpec((tm, LANES), lambda i: (i % seq_blocks, 0)),
            pl.BlockSpec((LANES, tm), lambda i: (0, i % seq_blocks)),
            pl.BlockSpec((LANES, tm), lambda i: (0, i % seq_blocks)),
        ],
        out_specs=[
            pl.BlockSpec((1, GQA_Q_HEADS * LANES, tm), lambda i: (i, 0, 0)),
            pl.BlockSpec((tm, GQA_KV_HEADS * LANES), lambda i: (i, 0)),
            pl.BlockSpec((1, GQA_KV_HEADS * FA_VROWS, tm), lambda i: (i, 0, 0)),
        ],
        out_shape=[
            jax.ShapeDtypeStruct((t // tm, GQA_Q_HEADS * LANES, tm), BF16),
            jax.ShapeDtypeStruct((t, GQA_KV_HEADS * LANES), BF16),
            jax.ShapeDtypeStruct((t // tm, GQA_KV_HEADS * FA_VROWS, tm), BF16),
        ],
        compiler_params=pltpu.CompilerParams(
            dimension_semantics=("parallel",), vmem_limit_bytes=VMEM_LIMIT),
        name="gqa_qkv",
    )(h, g.reshape(1, d), wqt, wkp, wvt, qg, kg, cos, sin, cost, sint)


def _flash_kernel(qt_ref, k_ref, vt_ref, o_ref, m_ref, acc_ref):
    m_ref[...] = jnp.full(m_ref.shape, -jnp.inf, F32)
    acc_ref[...] = jnp.zeros_like(acc_ref)

    n_kv = SEQ // FA_TK

    def scores(j, g):
        kj = k_ref[pl.ds(pl.multiple_of(j * FA_TK, FA_TK), FA_TK), :]
        return _dot(kj, qt_ref[0, g * LANES:(g + 1) * LANES, :])

    def body(j, s):
        vtj = vt_ref[j]
        for g in range(GQA_GROUPS):
            if g + 1 < GQA_GROUPS:
                s_next = scores(j, g + 1)
            else:
                s_next = scores(jnp.minimum(j + 1, n_kv - 1), 0)
            m_prev = m_ref[g]
            m_new = jnp.maximum(m_prev, jnp.max(s, axis=0, keepdims=True))
            p = jnp.exp2(s - m_new)
            acc_ref[g] = jnp.exp2(m_prev - m_new) * acc_ref[g] + _dot(vtj, p.astype(BF16))
            m_ref[g] = m_new
            s = s_next
        return s

    lax.fori_loop(0, n_kv, body, scores(0, 0))
    for pair in range(GQA_GROUPS // 2):
        halves = []
        for g in (2 * pair, 2 * pair + 1):
            acc = acc_ref[g]
            halves.append(acc[:HEAD_DIM] / acc[HEAD_DIM:HEAD_DIM + 1])
        o_ref[:, pair * LANES:(pair + 1) * LANES] = jnp.concatenate(halves, axis=0).T.astype(o_ref.dtype)


def flash_attention(qt, k, vt):
    q_blocks = SEQ // FA_TQ
    q_per_tile = FA_TK // FA_TQ
    kv_blocks = SEQ // FA_TK
    gw = GQA_GROUPS * LANES
    return pl.pallas_call(
        _flash_kernel,
        grid=(BATCH, GQA_KV_HEADS, q_blocks),
        in_specs=[
            pl.BlockSpec((1, gw, FA_TQ), lambda b, kh, i: (b * kv_blocks + i // q_per_tile, kh, i % q_per_tile)),
            pl.BlockSpec((SEQ, LANES), lambda b, kh, i: (b, kh)),
            pl.BlockSpec((kv_blocks, FA_VROWS, FA_TK), lambda b, kh, i: (b, kh, 0)),
        ],
        out_specs=pl.BlockSpec((FA_TQ, GQA_GROUPS * HEAD_DIM), lambda b, kh, i: (b * q_blocks + i, kh)),
        out_shape=jax.ShapeDtypeStruct((TOKENS, D_MODEL), BF16),
        scratch_shapes=[
            pltpu.VMEM((GQA_GROUPS, 1, FA_TQ), F32),
            pltpu.VMEM((GQA_GROUPS, FA_VROWS, FA_TQ), F32),
        ],
        compiler_params=pltpu.CompilerParams(
            dimension_semantics=("parallel", "parallel", "parallel"), vmem_limit_bytes=VMEM_LIMIT),
        name="gqa_flash",
    )(qt, k, vt)


def _moe_kernel(h_ref, gin_ref, wr_ref, wg_ref, wu_ref, wd_ref, gout_ref, o_ref,
                xn_ref, gates_ref, ecol_ref, eacc_ref, acc_ref):
    e = pl.program_id(1)
    j = pl.program_id(2)
    last_j = pl.num_programs(2) - 1
    lane = lax.broadcasted_iota(jnp.int32, (1, LANES), 1)

    @pl.when(jnp.logical_and(e == 0, j == 0))
    def _():
        x = _rms(h_ref[...], gin_ref[...])
        xn_ref[...] = x.astype(BF16)
        logits = jnp.dot(x, wr_ref[...], preferred_element_type=F32, precision=lax.Precision.HIGHEST)
        logits = jnp.where(lane < N_EXPERTS, logits, -jnp.inf)
        m1 = jnp.max(logits, axis=-1, keepdims=True)
        i1 = jnp.min(jnp.where(logits == m1, lane, LANES), axis=-1, keepdims=True)
        rest = jnp.where(lane == i1, -jnp.inf, logits)
        m2 = jnp.max(rest, axis=-1, keepdims=True)
        i2 = jnp.min(jnp.where(rest == m2, lane, LANES), axis=-1, keepdims=True)
        e2 = jnp.exp(m2 - m1)
        denom = 1.0 + e2
        gates_ref[...] = jnp.where(lane == i1, 1.0 / denom, 0.0) + jnp.where(lane == i2, e2 / denom, 0.0)
        acc_ref[...] = jnp.zeros_like(acc_ref)

    @pl.when(j == 0)
    def _():
        col = jnp.sum(jnp.where(lane == e, gates_ref[...], 0.0), axis=-1, keepdims=True)
        ecol_ref[...] = jnp.broadcast_to(col, ecol_ref.shape)
        eacc_ref[...] = jnp.zeros_like(eacc_ref)

    xn = xn_ref[...]
    gate = _dot(xn, wg_ref[0])
    up = _dot(xn, wu_ref[0])
    act = (gate * jax.nn.sigmoid(gate) * up).astype(BF16)
    eacc_ref[...] += _dot(act, wd_ref[0])

    @pl.when(j == last_j)
    def _():
        acc_ref[...] += ecol_ref[:, :1] * eacc_ref[...]

    @pl.when(jnp.logical_and(e == pl.num_programs(1) - 1, j == last_j))
    def _():
        o_ref[...] = h_ref[...] + _rms(acc_ref[...], gout_ref[...])


def moe_block(h, g_in, w_router, wg, wu, wd, g_out, tm=1024, tf=512):
    t, d = h.shape
    n_e, _, f = wg.shape
    wr = jnp.pad(w_router, ((0, 0), (0, LANES - n_e)))
    return pl.pallas_call(
        _moe_kernel,
        grid=(t // tm, n_e, f // tf),
        in_specs=[
            pl.BlockSpec((tm, d), lambda i, e, j: (i, 0)),
            pl.BlockSpec((1, d), lambda i, e, j: (0, 0)),
            pl.BlockSpec((d, LANES), lambda i, e, j: (0, 0)),
            pl.BlockSpec((1, d, tf), lambda i, e, j: (e, 0, j)),
            pl.BlockSpec((1, d, tf), lambda i, e, j: (e, 0, j)),
            pl.BlockSpec((1, tf, d), lambda i, e, j: (e, j, 0)),
            pl.BlockSpec((1, d), lambda i, e, j: (0, 0)),
        ],
        out_specs=pl.BlockSpec((tm, d), lambda i, e, j: (i, 0)),
        out_shape=jax.ShapeDtypeStruct((t, d), F32),
        scratch_shapes=[
            pltpu.VMEM((tm, d), BF16),
            pltpu.VMEM((tm, LANES), F32),
            pltpu.VMEM((tm, LANES), F32),
            pltpu.VMEM((tm, d), F32),
            pltpu.VMEM((tm, d), F32),
        ],
        compiler_params=pltpu.CompilerParams(
            dimension_semantics=("parallel", "arbitrary", "arbitrary"), vmem_limit_bytes=VMEM_LIMIT),
        name="moe_swiglu",
    )(h, g_in.reshape(1, d), wr, wg, wu, wd, g_out.reshape(1, d))


def kernel(x, norm_g, na_w_qkv, na_rpb, na_w_o, gqa_w_qkv, gqa_q_norm, gqa_k_norm, gqa_w_o,
           ffn_w_gate, ffn_w_up, ffn_w_down, moe_w_router, moe_w_gate, moe_w_up, moe_w_down):
    assert x.shape == (BATCH, SEQ, D_MODEL)
    h = x.reshape(TOKENS, D_MODEL)

    w_qkv = na_w_qkv[0].at[:, :D_MODEL].multiply(HEAD_DIM ** -0.5).astype(BF16)
    qkv = norm_matmul(h, norm_g[0, 0], w_qkv)
    a = neighborhood_attention(qkv, _na_bias_tables(na_rpb[0]))
    h = proj_norm_res(a, na_w_o[0].astype(BF16), norm_g[0, 1], h)
    h = ffn_block(h, norm_g[0, 2], ffn_w_gate[0].astype(BF16), ffn_w_up[0].astype(BF16),
                  ffn_w_down[0].astype(BF16), norm_g[0, 3])

    nq = GQA_Q_HEADS * HEAD_DIM
    nkv = GQA_KV_HEADS * HEAD_DIM
    wq, wk, wv = gqa_w_qkv[0][:, :nq], gqa_w_qkv[0][:, nq:nq + nkv], gqa_w_qkv[0][:, nq + nkv:]
    qt, k, vt = gqa_qkv(h, norm_g[1, 0], wq, wk, wv, gqa_q_norm[0], gqa_k_norm[0])
    o = flash_attention(qt, k, vt)
    h = proj_norm_res(o, gqa_w_o[0].astype(BF16), norm_g[1, 1], h)
    h = moe_block(h, norm_g[1, 2], moe_w_router[0], moe_w_gate[0].astype(BF16), moe_w_up[0].astype(BF16),
                  moe_w_down[0].astype(BF16), norm_g[1, 3])
    return h.reshape(BATCH, SEQ, D_MODEL)
```

```python
import functools

import jax
import jax.numpy as jnp
import numpy as np
from jax import lax
from jax.experimental import pallas as pl
from jax.experimental.pallas import tpu as pltpu

F32 = jnp.float32
BF16 = jnp.bfloat16

D_MODEL = 1024
BATCH = 2
SEQ = 8192
TOKENS = BATCH * SEQ
GRID_W = 64
GRID_H = SEQ // GRID_W
NA_HEADS = 16
HEAD_DIM = 64
WIN_H = 8
WIN_W = 16
GQA_Q_HEADS = 16
GQA_KV_HEADS = 4
GQA_GROUPS = GQA_Q_HEADS // GQA_KV_HEADS
ROPE_THETA = 10000.0
D_FF = 3584
N_EXPERTS = 8
EPS = 1e-6
LANES = 128
NEG_BIG = -1e30

VMEM_LIMIT = 56 * 1024 * 1024

NA_QROWS = 4
NA_QTOK = NA_QROWS * GRID_W
NA_BAND_BLOCKS = 3
NA_KTOK = NA_BAND_BLOCKS * NA_QTOK
NA_ROW_BLOCKS = GRID_H // NA_QROWS
NA_HEADS_PER_STEP = 4

FA_TQ = 512
FA_TK = 512
MOE_TM = 1024
MOE_TF = 896
MOE_GROUP = 384

FA_VROWS = 80


def _rms(x, g):
    ms = jnp.mean(x * x, axis=-1, keepdims=True)
    return x * lax.rsqrt(ms + EPS) * g


def _dot(a, b):
    return jnp.dot(a, b, preferred_element_type=F32)


def _dot_nt(a, b):
    return lax.dot_general(a, b, (((1,), (1,)), ((), ())), preferred_element_type=F32)


def _norm_matmul_kernel(x_ref, g_ref, w_ref, o_ref, *, n_chunk):
    xn = _rms(x_ref[...], g_ref[...]).astype(BF16)
    n = o_ref.shape[1]
    for n0 in range(0, n, n_chunk):
        o_ref[:, n0:n0 + n_chunk] = _dot(xn, w_ref[:, n0:n0 + n_chunk]).astype(o_ref.dtype)


def norm_matmul(x, g, w, tm=512, n_chunk=512):
    t, d = x.shape
    n = w.shape[1]
    return pl.pallas_call(
        functools.partial(_norm_matmul_kernel, n_chunk=n_chunk),
        grid=(t // tm,),
        in_specs=[
            pl.BlockSpec((tm, d), lambda i: (i, 0)),
            pl.BlockSpec((1, d), lambda i: (0, 0)),
            pl.BlockSpec((d, n), lambda i: (0, 0)),
        ],
        out_specs=pl.BlockSpec((tm, n), lambda i: (i, 0)),
        out_shape=jax.ShapeDtypeStruct((t, n), BF16),
        compiler_params=pltpu.CompilerParams(
            dimension_semantics=("parallel",), vmem_limit_bytes=VMEM_LIMIT),
        name="norm_matmul",
    )(x, g.reshape(1, d), w)


def _proj_norm_res_kernel(a_ref, w_ref, g_ref, h_ref, o_ref):
    y = _dot(a_ref[...], w_ref[...])
    o_ref[...] = h_ref[...] + _rms(y, g_ref[...])


def proj_norm_res(a, w, g, h, tm=512):
    t, k = a.shape
    d = w.shape[1]
    return pl.pallas_call(
        _proj_norm_res_kernel,
        grid=(t // tm,),
        in_specs=[
            pl.BlockSpec((tm, k), lambda i: (i, 0)),
            pl.BlockSpec((k, d), lambda i: (0, 0)),
            pl.BlockSpec((1, d), lambda i: (0, 0)),
            pl.BlockSpec((tm, d), lambda i: (i, 0)),
        ],
        out_specs=pl.BlockSpec((tm, d), lambda i: (i, 0)),
        out_shape=jax.ShapeDtypeStruct((t, d), F32),
        compiler_params=pltpu.CompilerParams(
            dimension_semantics=("parallel",), vmem_limit_bytes=VMEM_LIMIT),
        name="proj_norm_res",
    )(a, w, g.reshape(1, d), h)


def _na_bias_tables(rpb):
    i = np.arange(NA_QROWS)[:, None]
    m = np.arange(NA_BAND_BLOCKS * NA_QROWS)[None, :]
    ridx, rvalid = [], []
    for rb in (0, 1, NA_ROW_BLOCKS - 1):
        band0 = NA_QROWS * min(max(rb - 1, 0), NA_ROW_BLOCKS - NA_BAND_BLOCKS)
        r = NA_QROWS * rb + i
        kr = band0 + m
        rs = np.clip(r - WIN_H // 2, 0, GRID_H - WIN_H)
        rvalid.append((kr >= rs) & (kr < rs + WIN_H))
        ridx.append(np.clip(kr - r + WIN_H - 1, 0, 2 * WIN_H - 2))
    n_dr = 2 * WIN_H - 1
    plane = np.where(np.stack(rvalid), np.stack(ridx), n_dr).reshape(-1)
    c = np.arange(GRID_W)[:, None]
    kc = np.arange(GRID_W)[None, :]
    cs = np.clip(c - WIN_W // 2, 0, GRID_W - WIN_W)
    cvalid = (kc >= cs) & (kc < cs + WIN_W)
    left = GRID_W - WIN_W
    padded = jnp.pad(rpb, ((0, 0), (0, 0), (left, left)))
    cols = jnp.stack([padded[:, :, GRID_W - 1 - cq:2 * GRID_W - 1 - cq] for cq in range(GRID_W)], axis=2)
    cols = jnp.where(cvalid[None, None], cols, NEG_BIG)
    cols = jnp.concatenate([cols, jnp.full_like(cols[:, :1], NEG_BIG)], axis=1)
    tab = jnp.concatenate([cols[:, u:u + 1] for u in plane], axis=1)
    tab = tab.reshape(NA_HEADS, 3, NA_QROWS, NA_BAND_BLOCKS * NA_QROWS, GRID_W, GRID_W)
    return tab.transpose(1, 0, 2, 4, 3, 5).reshape(3, NA_HEADS, NA_QTOK, NA_KTOK)


def _na_kernel(q_ref, k0_ref, k1_ref, k2_ref, v0_ref, v1_ref, v2_ref, bias_ref, o_ref):
    lane = lax.broadcasted_iota(jnp.int32, (1, LANES), 1)
    low = lane < HEAD_DIM
    for pair in range(NA_HEADS_PER_STEP // 2):
        sl = slice(pair * LANES, (pair + 1) * LANES)
        q = q_ref[:, sl]
        k = jnp.concatenate([k0_ref[:, sl], k1_ref[:, sl], k2_ref[:, sl]], axis=0)
        v = jnp.concatenate([v0_ref[:, sl], v1_ref[:, sl], v2_ref[:, sl]], axis=0)
        outs = []
        for half in range(2):
            keep = low if half == 0 else jnp.logical_not(low)
            qh = jnp.where(keep, q, jnp.zeros_like(q))
            s = _dot_nt(qh, k) + bias_ref[0, 2 * pair + half]
            mx = jnp.max(s, axis=-1, keepdims=True)
            p = jnp.exp(s - mx)
            l = jnp.sum(p, axis=-1, keepdims=True)
            outs.append(_dot(p.astype(BF16), v) / l)
        o_ref[:, sl] = jnp.where(low, outs[0], outs[1]).astype(o_ref.dtype)


def neighborhood_attention(qkv, bias_tab):
    hw = NA_HEADS_PER_STEP * HEAD_DIM
    nh = D_MODEL // hw
    blocks_per_seq = SEQ // NA_QTOK

    def band(rb):
        return jnp.clip(rb - 1, 0, NA_ROW_BLOCKS - NA_BAND_BLOCKS)

    def case(rb):
        return jnp.where(rb == 0, 0, jnp.where(rb == NA_ROW_BLOCKS - 1, 2, 1))

    q_spec = pl.BlockSpec((NA_QTOK, hw), lambda h, b, rb: (b * blocks_per_seq + rb, h))
    kv_specs = [
        pl.BlockSpec((NA_QTOK, hw),
                     lambda h, b, rb, j=j, part=part: (b * blocks_per_seq + band(rb) + j, part * nh + h))
        for part in (1, 2) for j in range(NA_BAND_BLOCKS)
    ]
    bias_spec = pl.BlockSpec((1, NA_HEADS_PER_STEP, NA_QTOK, NA_KTOK),
                             lambda h, b, rb: (case(rb), h, 0, 0))
    return pl.pallas_call(
        _na_kernel,
        grid=(nh, BATCH, NA_ROW_BLOCKS),
        in_specs=[q_spec] + kv_specs + [bias_spec],
        out_specs=pl.BlockSpec((NA_QTOK, hw), lambda h, b, rb: (b * blocks_per_seq + rb, h)),
        out_shape=jax.ShapeDtypeStruct((TOKENS, D_MODEL), BF16),
        compiler_params=pltpu.CompilerParams(
            dimension_semantics=("parallel", "parallel", "parallel"), vmem_limit_bytes=VMEM_LIMIT),
        name="na_attention",
    )(qkv, qkv, qkv, qkv, qkv, qkv, qkv, bias_tab)


def _ffn_kernel(h_ref, gin_ref, wg_ref, wu_ref, wd_ref, gout_ref, o_ref, xn_ref, acc_ref):
    j = pl.program_id(1)

    @pl.when(j == 0)
    def _():
        xn_ref[...] = _rms(h_ref[...], gin_ref[...]).astype(BF16)
        acc_ref[...] = jnp.zeros_like(acc_ref)

    xn = xn_ref[...]
    gate = _dot(xn, wg_ref[...])
    up = _dot(xn, wu_ref[...])
    act = (gate * jax.nn.sigmoid(gate) * up).astype(BF16)
    acc_ref[...] += _dot(act, wd_ref[...])

    @pl.when(j == pl.num_programs(1) - 1)
    def _():
        o_ref[...] = h_ref[...] + _rms(acc_ref[...], gout_ref[...])


def ffn_block(h, g_in, wg, wu, wd, g_out, tm=1024, tf=512):
    t, d = h.shape
    f = wg.shape[1]
    return pl.pallas_call(
        _ffn_kernel,
        grid=(t // tm, f // tf),
        in_specs=[
            pl.BlockSpec((tm, d), lambda i, j: (i, 0)),
            pl.BlockSpec((1, d), lambda i, j: (0, 0)),
            pl.BlockSpec((d, tf), lambda i, j: (0, j)),
            pl.BlockSpec((d, tf), lambda i, j: (0, j)),
            pl.BlockSpec((tf, d), lambda i, j: (j, 0)),
            pl.BlockSpec((1, d), lambda i, j: (0, 0)),
        ],
        out_specs=pl.BlockSpec((tm, d), lambda i, j: (i, 0)),
        out_shape=jax.ShapeDtypeStruct((t, d), F32),
        scratch_shapes=[pltpu.VMEM((tm, d), BF16), pltpu.VMEM((tm, d), F32)],
        compiler_params=pltpu.CompilerParams(
            dimension_semantics=("parallel", "arbitrary"), vmem_limit_bytes=VMEM_LIMIT),
        name="ffn_swiglu",
    )(h, g_in.reshape(1, d), wg, wu, wd, g_out.reshape(1, d))


def _pad_heads(w, n_heads, width):
    d = w.shape[0]
    w = w.reshape(d, n_heads, HEAD_DIM)
    return jnp.pad(w, ((0, 0), (0, 0), (0, width - HEAD_DIM))).reshape(d, n_heads * width)


def _rope_tables():
    quarter = HEAD_DIM // 4
    freqs = ROPE_THETA ** (-jnp.arange(quarter, dtype=F32) / quarter)
    t = jnp.arange(SEQ)
    row_ang = (t // GRID_W).astype(F32)[:, None] * freqs[None, :]
    col_ang = (t % GRID_W).astype(F32)[:, None] * freqs[None, :]
    zeros = jnp.zeros((SEQ, LANES - HEAD_DIM), F32)
    cos = jnp.concatenate([jnp.cos(row_ang), jnp.cos(row_ang), jnp.cos(col_ang), jnp.cos(col_ang), zeros], axis=1)
    sin = jnp.concatenate([-jnp.sin(row_ang), jnp.sin(row_ang), -jnp.sin(col_ang), jnp.sin(col_ang), zeros], axis=1)
    return cos, sin


def _gqa_qkv_kernel(x_ref, g_ref, wqt_ref, wk_ref, wvt_ref, qg_ref, kg_ref, cos_ref, sin_ref, cost_ref, sint_ref,
                    qt_ref, k_ref, vt_ref):
    xn = _rms(x_ref[...], g_ref[...]).astype(BF16)
    quarter = HEAD_DIM // 4

    cost = cost_ref[...]
    sint = sint_ref[...]
    qg = qg_ref[...]
    for hh in range(GQA_Q_HEADS):
        y = _dot_nt(wqt_ref[hh * LANES:(hh + 1) * LANES, :], xn)
        ms = jnp.sum(y * y, axis=0, keepdims=True) * (1.0 / HEAD_DIM)
        y = y * lax.rsqrt(ms + EPS) * qg
        partner = jnp.concatenate(
            [y[quarter:2 * quarter], y[:quarter], y[3 * quarter:HEAD_DIM], y[2 * quarter:3 * quarter], y[HEAD_DIM:]],
            axis=0)
        qt_ref[0, hh * LANES:(hh + 1) * LANES, :] = (y * cost + partner * sint).astype(BF16)

    lane = lax.broadcasted_iota(jnp.int32, (1, LANES), 1)
    first = (lane % (2 * quarter)) < quarter
    cos = cos_ref[...]
    sin = sin_ref[...]
    row = lax.broadcasted_iota(jnp.int32, (FA_VROWS, 1), 0)
    for hh in range(GQA_KV_HEADS):
        y = _dot(xn, wk_ref[:, hh * LANES:(hh + 1) * LANES])
        ms = jnp.sum(y * y, axis=-1, keepdims=True) * (1.0 / HEAD_DIM)
        y = y * lax.rsqrt(ms + EPS) * kg_ref[...]
        partner = jnp.where(first, pltpu.roll(y, LANES - quarter, 1), pltpu.roll(y, quarter, 1))
        k_ref[:, hh * LANES:(hh + 1) * LANES] = (y * cos + partner * sin).astype(BF16)
        vt = _dot_nt(wvt_ref[hh * FA_VROWS:(hh + 1) * FA_VROWS, :], xn)
        vt_ref[0, hh * FA_VROWS:(hh + 1) * FA_VROWS, :] = jnp.where(row == HEAD_DIM, 1.0, vt).astype(BF16)


def gqa_qkv(h, g, wq, wk, wv, q_gain, k_gain):
    t, d = h.shape
    tm = FA_TK
    seq_blocks = SEQ // tm
    wqt = _pad_heads(wq, GQA_Q_HEADS, LANES).T.astype(BF16)
    wkp = _pad_heads(wk, GQA_KV_HEADS, LANES).astype(BF16)
    wvt = _pad_heads(wv, GQA_KV_HEADS, FA_VROWS).T.astype(BF16)
    cos, sin = _rope_tables()
    q_scale = (HEAD_DIM ** -0.5) * float(np.log2(np.e))
    cost, sint = (cos * q_scale).T, (sin * q_scale).T
    qg = jnp.pad(q_gain, (0, LANES - HEAD_DIM)).reshape(LANES, 1)
    kg = jnp.pad(k_gain, (0, LANES - HEAD_DIM)).reshape(1, LANES)
    const = lambda shape: pl.BlockSpec(shape, lambda i: (0,) * len(shape))
    return pl.pallas_call(
        _gqa_qkv_kernel,
        grid=(t // tm,),
        in_specs=[
            pl.BlockSpec((tm, d), lambda i: (i, 0)),
            const((1, d)),
            const(wqt.shape),
            const(wkp.shape),
            const(wvt.shape),
            const((LANES, 1)),
            const((1, LANES)),
            pl.BlockSpec((tm, LANES), lambda i: (i % seq_blocks, 0)),
            pl.BlockSpec((tm, LANES), lambda i: (i % seq_blocks, 0)),
            pl.BlockSpec((LANES, tm), lambda i: (0, i % seq_blocks)),
            pl.BlockSpec((LANES, tm), lambda i: (0, i % seq_blocks)),
        ],
        out_specs=[
            pl.BlockSpec((1, GQA_Q_HEADS * LANES, tm), lambda i: (i, 0, 0)),
            pl.BlockSpec((tm, GQA_KV_HEADS * LANES), lambda i: (i, 0)),
            pl.BlockSpec((1, GQA_KV_HEADS * FA_VROWS, tm), lambda i: (i, 0, 0)),
        ],
        out_shape=[
            jax.ShapeDtypeStruct((t // tm, GQA_Q_HEADS * LANES, tm), BF16),
            jax.ShapeDtypeStruct((t, GQA_KV_HEADS * LANES), BF16),
            jax.ShapeDtypeStruct((t // tm, GQA_KV_HEADS * FA_VROWS, tm), BF16),
        ],
        compiler_params=pltpu.CompilerParams(
            dimension_semantics=("parallel",), vmem_limit_bytes=VMEM_LIMIT),
        name="gqa_qkv",
    )(h, g.reshape(1, d), wqt, wkp, wvt, qg, kg, cos, sin, cost, sint)


def _flash_kernel(qt_ref, k_ref, vt_ref, o_ref, m_ref, acc_ref):
    m_ref[...] = jnp.full(m_ref.shape, -jnp.inf, F32)
    acc_ref[...] = jnp.zeros_like(acc_ref)

    n_kv = SEQ // FA_TK

    def scores(j, g):
        kj = k_ref[pl.ds(pl.multiple_of(j * FA_TK, FA_TK), FA_TK), :]
        return _dot(kj, qt_ref[0, g * LANES:(g + 1) * LANES, :])

    def body(j, s):
        vtj = vt_ref[j]
        for g in range(GQA_GROUPS):
            if g + 1 < GQA_GROUPS:
                s_next = scores(j, g + 1)
            else:
                s_next = scores(jnp.minimum(j + 1, n_kv - 1), 0)
            m_prev = m_ref[g]
            m_new = jnp.maximum(m_prev, jnp.max(s, axis=0, keepdims=True))
            p = jnp.exp2(s - m_new)
            acc_ref[g] = jnp.exp2(m_prev - m_new) * acc_ref[g] + _dot(vtj, p.astype(BF16))
            m_ref[g] = m_new
            s = s_next
        return s

    lax.fori_loop(0, n_kv, body, scores(0, 0))
    for pair in range(GQA_GROUPS // 2):
        halves = []
        for g in (2 * pair, 2 * pair + 1):
            acc = acc_ref[g]
            halves.append(acc[:HEAD_DIM] / acc[HEAD_DIM:HEAD_DIM + 1])
        o_ref[:, pair * LANES:(pair + 1) * LANES] = jnp.concatenate(halves, axis=0).T.astype(o_ref.dtype)


def flash_attention(qt, k, vt):
    q_blocks = SEQ // FA_TQ
    q_per_tile = FA_TK // FA_TQ
    kv_blocks = SEQ // FA_TK
    gw = GQA_GROUPS * LANES
    return pl.pallas_call(
        _flash_kernel,
        grid=(BATCH, GQA_KV_HEADS, q_blocks),
        in_specs=[
            pl.BlockSpec((1, gw, FA_TQ), lambda b, kh, i: (b * kv_blocks + i // q_per_tile, kh, i % q_per_tile)),
            pl.BlockSpec((SEQ, LANES), lambda b, kh, i: (b, kh)),
            pl.BlockSpec((kv_blocks, FA_VROWS, FA_TK), lambda b, kh, i: (b, kh, 0)),
        ],
        out_specs=pl.BlockSpec((FA_TQ, GQA_GROUPS * HEAD_DIM), lambda b, kh, i: (b * q_blocks + i, kh)),
        out_shape=jax.ShapeDtypeStruct((TOKENS, D_MODEL), BF16),
        scratch_shapes=[
            pltpu.VMEM((GQA_GROUPS, 1, FA_TQ), F32),
            pltpu.VMEM((GQA_GROUPS, FA_VROWS, FA_TQ), F32),
        ],
        compiler_params=pltpu.CompilerParams(
            dimension_semantics=("parallel", "parallel", "parallel"), vmem_limit_bytes=VMEM_LIMIT),
        name="gqa_flash",
    )(qt, k, vt)


def _router_kernel(h_ref, gin_ref, wr_ref, xn_ref, gates_ref, rank_ref, rankt_ref, count_ref):
    tm = h_ref.shape[0]
    lane = lax.broadcasted_iota(jnp.int32, (1, LANES), 1)
    x = _rms(h_ref[...], gin_ref[...])
    xn_ref[...] = x.astype(BF16)
    logits = jnp.dot(x, wr_ref[...], preferred_element_type=F32, precision=lax.Precision.HIGHEST)
    logits = jnp.where(lane < N_EXPERTS, logits, -jnp.inf)
    m1 = jnp.max(logits, axis=-1, keepdims=True)
    i1 = jnp.min(jnp.where(logits == m1, lane, LANES), axis=-1, keepdims=True)
    rest = jnp.where(lane == i1, -jnp.inf, logits)
    m2 = jnp.max(rest, axis=-1, keepdims=True)
    i2 = jnp.min(jnp.where(rest == m2, lane, LANES), axis=-1, keepdims=True)
    e2 = jnp.exp(m2 - m1)
    denom = 1.0 + e2
    gates_ref[...] = jnp.where(lane == i1, 1.0 / denom, 0.0) + jnp.where(lane == i2, e2 / denom, 0.0)
    chosen = jnp.logical_or(lane == i1, lane == i2)
    earlier = lax.broadcasted_iota(jnp.int32, (tm, tm), 1) < lax.broadcasted_iota(jnp.int32, (tm, tm), 0)
    prefix = _dot(jnp.where(earlier, 1.0, 0.0).astype(BF16), jnp.where(chosen, 1.0, 0.0).astype(BF16))
    rank = jnp.where(chosen, prefix, -1.0)
    rank_ref[...] = rank
    rankt_ref[0] = rank.T
    count_ref[0] = jnp.sum(jnp.where(chosen, 1.0, 0.0), axis=0, keepdims=True)


def _moe_kernel(ng_ref, xn_ref, gates_ref, rank_ref, rankt_ref, wg_ref, wu_ref, wd_ref, h_ref, gout_ref, o_ref,
                xs_ref, yacc_ref, acc_ref):
    i = pl.program_id(0)
    e = pl.program_id(1)
    j = pl.program_id(2)
    last_e = pl.num_programs(1) - 1
    last_j = pl.num_programs(2) - 1
    tm = xn_ref.shape[0]
    n_groups = ng_ref[i * N_EXPERTS + e]

    @pl.when(jnp.logical_and(e == 0, j == 0))
    def _():
        acc_ref[...] = jnp.zeros_like(acc_ref)

    @pl.when(j == 0)
    def _():
        rank_row = rankt_ref[0, pl.ds(e, 1), :]

        def compact(c, carry):
            slot = (lax.broadcasted_iota(jnp.int32, (MOE_GROUP, 1), 0) + c * MOE_GROUP).astype(F32)
            onehot = jnp.where(rank_row == slot, 1.0, 0.0).astype(BF16)
            xs_ref[c] = _dot(onehot, xn_ref[...]).astype(BF16)
            yacc_ref[c] = jnp.zeros(yacc_ref.shape[1:], F32)
            return carry

        lax.fori_loop(0, n_groups, compact, 0)

    def expert(c, carry):
        xs = xs_ref[c]
        gate = _dot(xs, wg_ref[0])
        up = _dot(xs, wu_ref[0])
        act = (gate * jax.nn.sigmoid(gate) * up).astype(BF16)
        yacc_ref[c] += _dot(act, wd_ref[0])
        return carry

    lax.fori_loop(0, n_groups, expert, 0)

    @pl.when(j == last_j)
    def _():
        lane = lax.broadcasted_iota(jnp.int32, (1, LANES), 1)
        rank_col = jnp.sum(jnp.where(lane == e, rank_ref[...], 0.0), axis=-1, keepdims=True)
        gate_col = jnp.sum(jnp.where(lane == e, gates_ref[...], 0.0), axis=-1, keepdims=True)

        def expand(c, carry):
            slot = (lax.broadcasted_iota(jnp.int32, (1, MOE_GROUP), 1) + c * MOE_GROUP).astype(F32)
            onehot = jnp.where(rank_col == slot, 1.0, 0.0).astype(BF16)
            acc_ref[...] += gate_col * _dot(onehot, yacc_ref[c].astype(BF16))
            return carry

        lax.fori_loop(0, n_groups, expand, 0)

    @pl.when(jnp.logical_and(e == last_e, j == last_j))
    def _():
        o_ref[...] = h_ref[...] + _rms(acc_ref[...], gout_ref[...])


def moe_block(h, g_in, w_router, wg, wu, wd, g_out, tm=MOE_TM, tf=MOE_TF):
    t, d = h.shape
    n_e, _, f = wg.shape
    n_tiles = t // tm
    wr = jnp.pad(w_router, ((0, 0), (0, LANES - n_e)))
    xn, gates, rank, rankt, counts = pl.pallas_call(
        _router_kernel,
        grid=(n_tiles,),
        in_specs=[
            pl.BlockSpec((tm, d), lambda i: (i, 0)),
            pl.BlockSpec((1, d), lambda i: (0, 0)),
            pl.BlockSpec((d, LANES), lambda i: (0, 0)),
        ],
        out_specs=[
            pl.BlockSpec((tm, d), lambda i: (i, 0)),
            pl.BlockSpec((tm, LANES), lambda i: (i, 0)),
            pl.BlockSpec((tm, LANES), lambda i: (i, 0)),
            pl.BlockSpec((1, LANES, tm), lambda i: (i, 0, 0)),
            pl.BlockSpec((1, 1, LANES), lambda i: (i, 0, 0)),
        ],
        out_shape=[
            jax.ShapeDtypeStruct((t, d), BF16),
            jax.ShapeDtypeStruct((t, LANES), F32),
            jax.ShapeDtypeStruct((t, LANES), F32),
            jax.ShapeDtypeStruct((n_tiles, LANES, tm), F32),
            jax.ShapeDtypeStruct((n_tiles, 1, LANES), F32),
        ],
        compiler_params=pltpu.CompilerParams(
            dimension_semantics=("parallel",), vmem_limit_bytes=VMEM_LIMIT),
        name="moe_router",
    )(h, g_in.reshape(1, d), wr)
    n_groups = (counts[:, 0, :n_e].astype(jnp.int32) + (MOE_GROUP - 1)) // MOE_GROUP
    max_groups = -(-tm // MOE_GROUP)
    grid_spec = pltpu.PrefetchScalarGridSpec(
        num_scalar_prefetch=1,
        grid=(n_tiles, n_e, f // tf),
        in_specs=[
            pl.BlockSpec((tm, d), lambda i, e, j, ng: (i, 0)),
            pl.BlockSpec((tm, LANES), lambda i, e, j, ng: (i, 0)),
            pl.BlockSpec((tm, LANES), lambda i, e, j, ng: (i, 0)),
            pl.BlockSpec((1, LANES, tm), lambda i, e, j, ng: (i, 0, 0)),
            pl.BlockSpec((1, d, tf), lambda i, e, j, ng: (e, 0, j)),
            pl.BlockSpec((1, d, tf), lambda i, e, j, ng: (e, 0, j)),
            pl.BlockSpec((1, tf, d), lambda i, e, j, ng: (e, j, 0)),
            pl.BlockSpec((tm, d), lambda i, e, j, ng: (i, 0)),
            pl.BlockSpec((1, d), lambda i, e, j, ng: (0, 0)),
        ],
        out_specs=pl.BlockSpec((tm, d), lambda i, e, j, ng: (i, 0)),
        scratch_shapes=[
            pltpu.VMEM((max_groups, MOE_GROUP, d), BF16),
            pltpu.VMEM((max_groups, MOE_GROUP, d), F32),
            pltpu.VMEM((tm, d), F32),
        ],
    )
    return pl.pallas_call(
        _moe_kernel,
        grid_spec=grid_spec,
        out_shape=jax.ShapeDtypeStruct((t, d), F32),
        compiler_params=pltpu.CompilerParams(
            dimension_semantics=("parallel", "arbitrary", "arbitrary"), vmem_limit_bytes=VMEM_LIMIT),
        name="moe_swiglu",
    )(n_groups.reshape(-1), xn, gates, rank, rankt, wg, wu, wd, h, g_out.reshape(1, d))


def kernel(x, norm_g, na_w_qkv, na_rpb, na_w_o, gqa_w_qkv, gqa_q_norm, gqa_k_norm, gqa_w_o,
           ffn_w_gate, ffn_w_up, ffn_w_down, moe_w_router, moe_w_gate, moe_w_up, moe_w_down):
    assert x.shape == (BATCH, SEQ, D_MODEL)
    h = x.reshape(TOKENS, D_MODEL)

    w_qkv = na_w_qkv[0].at[:, :D_MODEL].multiply(HEAD_DIM ** -0.5).astype(BF16)
    qkv = norm_matmul(h, norm_g[0, 0], w_qkv)
    a = neighborhood_attention(qkv, _na_bias_tables(na_rpb[0]))
    h = proj_norm_res(a, na_w_o[0].astype(BF16), norm_g[0, 1], h)
    h = ffn_block(h, norm_g[0, 2], ffn_w_gate[0].astype(BF16), ffn_w_up[0].astype(BF16),
                  ffn_w_down[0].astype(BF16), norm_g[0, 3])

    nq = GQA_Q_HEADS * HEAD_DIM
    nkv = GQA_KV_HEADS * HEAD_DIM
    wq, wk, wv = gqa_w_qkv[0][:, :nq], gqa_w_qkv[0][:, nq:nq + nkv], gqa_w_qkv[0][:, nq + nkv:]
    qt, k, vt = gqa_qkv(h, norm_g[1, 0], wq, wk, wv, gqa_q_norm[0], gqa_k_norm[0])
    o = flash_attention(qt, k, vt)
    h = proj_norm_res(o, gqa_w_o[0].astype(BF16), norm_g[1, 1], h)
    h = moe_block(h, norm_g[1, 2], moe_w_router[0], moe_w_gate[0].astype(BF16), moe_w_up[0].astype(BF16),
                  moe_w_down[0].astype(BF16), norm_g[1, 3])
    return h.reshape(BATCH, SEQ, D_MODEL)
```

```python
import functools

import jax
import jax.numpy as jnp
import numpy as np
from jax import lax
from jax.experimental import pallas as pl
from jax.experimental.pallas import tpu as pltpu

F32 = jnp.float32
BF16 = jnp.bfloat16

D_MODEL = 1024
BATCH = 2
SEQ = 8192
TOKENS = BATCH * SEQ
GRID_W = 64
GRID_H = SEQ // GRID_W
NA_HEADS = 16
HEAD_DIM = 64
WIN_H = 8
WIN_W = 16
GQA_Q_HEADS = 16
GQA_KV_HEADS = 4
GQA_GROUPS = GQA_Q_HEADS // GQA_KV_HEADS
ROPE_THETA = 10000.0
D_FF = 3584
N_EXPERTS = 8
EPS = 1e-6
LANES = 128
NEG_BIG = -1e30

VMEM_LIMIT = 56 * 1024 * 1024

NA_QROWS = 4
NA_QTOK = NA_QROWS * GRID_W
NA_BAND_BLOCKS = 3
NA_KTOK = NA_BAND_BLOCKS * NA_QTOK
NA_ROW_BLOCKS = GRID_H // NA_QROWS
NA_HEADS_PER_STEP = 4

FA_TQ = 512
FA_TK = 512
MOE_TM = 1024
MOE_TF = 896
MOE_GROUP = 384

FA_UNROLL = 3
FA_MAX_JUMP = 24.0
FA_VROWS = 80


def _rms(x, g):
    ms = jnp.mean(x * x, axis=-1, keepdims=True)
    return x * lax.rsqrt(ms + EPS) * g


def _dot(a, b):
    return jnp.dot(a, b, preferred_element_type=F32)


def _dot_nt(a, b):
    return lax.dot_general(a, b, (((1,), (1,)), ((), ())), preferred_element_type=F32)


def _norm_matmul_kernel(x_ref, g_ref, w_ref, o_ref, *, n_chunk):
    xn = _rms(x_ref[...], g_ref[...]).astype(BF16)
    n = o_ref.shape[1]
    for n0 in range(0, n, n_chunk):
        o_ref[:, n0:n0 + n_chunk] = _dot(xn, w_ref[:, n0:n0 + n_chunk]).astype(o_ref.dtype)


def norm_matmul(x, g, w, tm=512, n_chunk=512):
    t, d = x.shape
    n = w.shape[1]
    return pl.pallas_call(
        functools.partial(_norm_matmul_kernel, n_chunk=n_chunk),
        grid=(t // tm,),
        in_specs=[
            pl.BlockSpec((tm, d), lambda i: (i, 0)),
            pl.BlockSpec((1, d), lambda i: (0, 0)),
            pl.BlockSpec((d, n), lambda i: (0, 0)),
        ],
        out_specs=pl.BlockSpec((tm, n), lambda i: (i, 0)),
        out_shape=jax.ShapeDtypeStruct((t, n), BF16),
        compiler_params=pltpu.CompilerParams(
            dimension_semantics=("parallel",), vmem_limit_bytes=VMEM_LIMIT),
        name="norm_matmul",
    )(x, g.reshape(1, d), w)


def _proj_norm_res_kernel(a_ref, w_ref, g_ref, h_ref, o_ref):
    y = _dot(a_ref[...], w_ref[...])
    o_ref[...] = h_ref[...] + _rms(y, g_ref[...])


def proj_norm_res(a, w, g, h, tm=512):
    t, k = a.shape
    d = w.shape[1]
    return pl.pallas_call(
        _proj_norm_res_kernel,
        grid=(t // tm,),
        in_specs=[
            pl.BlockSpec((tm, k), lambda i: (i, 0)),
            pl.BlockSpec((k, d), lambda i: (0, 0)),
            pl.BlockSpec((1, d), lambda i: (0, 0)),
            pl.BlockSpec((tm, d), lambda i: (i, 0)),
        ],
        out_specs=pl.BlockSpec((tm, d), lambda i: (i, 0)),
        out_shape=jax.ShapeDtypeStruct((t, d), F32),
        compiler_params=pltpu.CompilerParams(
            dimension_semantics=("parallel",), vmem_limit_bytes=VMEM_LIMIT),
        name="proj_norm_res",
    )(a, w, g.reshape(1, d), h)


def _na_bias_tables(rpb):
    i = np.arange(NA_QROWS)[:, None]
    m = np.arange(NA_BAND_BLOCKS * NA_QROWS)[None, :]
    ridx, rvalid = [], []
    for rb in (0, 1, NA_ROW_BLOCKS - 1):
        band0 = NA_QROWS * min(max(rb - 1, 0), NA_ROW_BLOCKS - NA_BAND_BLOCKS)
        r = NA_QROWS * rb + i
        kr = band0 + m
        rs = np.clip(r - WIN_H // 2, 0, GRID_H - WIN_H)
        rvalid.append((kr >= rs) & (kr < rs + WIN_H))
        ridx.append(np.clip(kr - r + WIN_H - 1, 0, 2 * WIN_H - 2))
    n_dr = 2 * WIN_H - 1
    plane = np.where(np.stack(rvalid), np.stack(ridx), n_dr).reshape(-1)
    c = np.arange(GRID_W)[:, None]
    kc = np.arange(GRID_W)[None, :]
    cs = np.clip(c - WIN_W // 2, 0, GRID_W - WIN_W)
    cvalid = (kc >= cs) & (kc < cs + WIN_W)
    left = GRID_W - WIN_W
    padded = jnp.pad(rpb, ((0, 0), (0, 0), (left, left)))
    cols = jnp.stack([padded[:, :, GRID_W - 1 - cq:2 * GRID_W - 1 - cq] for cq in range(GRID_W)], axis=2)
    cols = jnp.where(cvalid[None, None], cols, NEG_BIG)
    cols = jnp.concatenate([cols, jnp.full_like(cols[:, :1], NEG_BIG)], axis=1)
    tab = jnp.concatenate([cols[:, u:u + 1] for u in plane], axis=1)
    tab = tab.reshape(NA_HEADS, 3, NA_QROWS, NA_BAND_BLOCKS * NA_QROWS, GRID_W, GRID_W)
    return tab.transpose(1, 0, 2, 4, 3, 5).reshape(3, NA_HEADS, NA_QTOK, NA_KTOK)


def _na_kernel(q_ref, k0_ref, k1_ref, k2_ref, v0_ref, v1_ref, v2_ref, bias_ref, o_ref):
    lane = lax.broadcasted_iota(jnp.int32, (1, LANES), 1)
    low = lane < HEAD_DIM
    for pair in range(NA_HEADS_PER_STEP // 2):
        sl = slice(pair * LANES, (pair + 1) * LANES)
        q = q_ref[:, sl]
        k = jnp.concatenate([k0_ref[:, sl], k1_ref[:, sl], k2_ref[:, sl]], axis=0)
        v = jnp.concatenate([v0_ref[:, sl], v1_ref[:, sl], v2_ref[:, sl]], axis=0)
        outs = []
        for half in range(2):
            keep = low if half == 0 else jnp.logical_not(low)
            qh = jnp.where(keep, q, jnp.zeros_like(q))
            s = _dot_nt(qh, k) + bias_ref[0, 2 * pair + half]
            mx = jnp.max(s, axis=-1, keepdims=True)
            p = jnp.exp(s - mx)
            l = jnp.sum(p, axis=-1, keepdims=True)
            outs.append(_dot(p.astype(BF16), v) / l)
        o_ref[:, sl] = jnp.where(low, outs[0], outs[1]).astype(o_ref.dtype)


def neighborhood_attention(qkv, bias_tab):
    hw = NA_HEADS_PER_STEP * HEAD_DIM
    nh = D_MODEL // hw
    blocks_per_seq = SEQ // NA_QTOK

    def band(rb):
        return jnp.clip(rb - 1, 0, NA_ROW_BLOCKS - NA_BAND_BLOCKS)

    def case(rb):
        return jnp.where(rb == 0, 0, jnp.where(rb == NA_ROW_BLOCKS - 1, 2, 1))

    q_spec = pl.BlockSpec((NA_QTOK, hw), lambda h, b, rb: (b * blocks_per_seq + rb, h))
    kv_specs = [
        pl.BlockSpec((NA_QTOK, hw),
                     lambda h, b, rb, j=j, part=part: (b * blocks_per_seq + band(rb) + j, part * nh + h))
        for part in (1, 2) for j in range(NA_BAND_BLOCKS)
    ]
    bias_spec = pl.BlockSpec((1, NA_HEADS_PER_STEP, NA_QTOK, NA_KTOK),
                             lambda h, b, rb: (case(rb), h, 0, 0))
    return pl.pallas_call(
        _na_kernel,
        grid=(nh, BATCH, NA_ROW_BLOCKS),
        in_specs=[q_spec] + kv_specs + [bias_spec],
        out_specs=pl.BlockSpec((NA_QTOK, hw), lambda h, b, rb: (b * blocks_per_seq + rb, h)),
        out_shape=jax.ShapeDtypeStruct((TOKENS, D_MODEL), BF16),
        compiler_params=pltpu.CompilerParams(
            dimension_semantics=("parallel", "parallel", "parallel"), vmem_limit_bytes=VMEM_LIMIT),
        name="na_attention",
    )(qkv, qkv, qkv, qkv, qkv, qkv, qkv, bias_tab)


def _ffn_kernel(h_ref, gin_ref, wg_ref, wu_ref, wd_ref, gout_ref, o_ref, xn_ref, acc_ref):
    j = pl.program_id(1)

    @pl.when(j == 0)
    def _():
        xn_ref[...] = _rms(h_ref[...], gin_ref[...]).astype(BF16)
        acc_ref[...] = jnp.zeros_like(acc_ref)

    xn = xn_ref[...]
    gate = _dot(xn, wg_ref[...])
    up = _dot(xn, wu_ref[...])
    act = (gate * jax.nn.sigmoid(gate) * up).astype(BF16)
    acc_ref[...] += _dot(act, wd_ref[...])

    @pl.when(j == pl.num_programs(1) - 1)
    def _():
        o_ref[...] = h_ref[...] + _rms(acc_ref[...], gout_ref[...])


def ffn_block(h, g_in, wg, wu, wd, g_out, tm=1024, tf=512):
    t, d = h.shape
    f = wg.shape[1]
    return pl.pallas_call(
        _ffn_kernel,
        grid=(t // tm, f // tf),
        in_specs=[
            pl.BlockSpec((tm, d), lambda i, j: (i, 0)),
            pl.BlockSpec((1, d), lambda i, j: (0, 0)),
            pl.BlockSpec((d, tf), lambda i, j: (0, j)),
            pl.BlockSpec((d, tf), lambda i, j: (0, j)),
            pl.BlockSpec((tf, d), lambda i, j: (j, 0)),
            pl.BlockSpec((1, d), lambda i, j: (0, 0)),
        ],
        out_specs=pl.BlockSpec((tm, d), lambda i, j: (i, 0)),
        out_shape=jax.ShapeDtypeStruct((t, d), F32),
        scratch_shapes=[pltpu.VMEM((tm, d), BF16), pltpu.VMEM((tm, d), F32)],
        compiler_params=pltpu.CompilerParams(
            dimension_semantics=("parallel", "arbitrary"), vmem_limit_bytes=VMEM_LIMIT),
        name="ffn_swiglu",
    )(h, g_in.reshape(1, d), wg, wu, wd, g_out.reshape(1, d))


def _pad_heads(w, n_heads, width):
    d = w.shape[0]
    w = w.reshape(d, n_heads, HEAD_DIM)
    return jnp.pad(w, ((0, 0), (0, 0), (0, width - HEAD_DIM))).reshape(d, n_heads * width)


def _rope_tables():
    quarter = HEAD_DIM // 4
    freqs = ROPE_THETA ** (-jnp.arange(quarter, dtype=F32) / quarter)
    t = jnp.arange(SEQ)
    row_ang = (t // GRID_W).astype(F32)[:, None] * freqs[None, :]
    col_ang = (t % GRID_W).astype(F32)[:, None] * freqs[None, :]
    zeros = jnp.zeros((SEQ, LANES - HEAD_DIM), F32)
    cos = jnp.concatenate([jnp.cos(row_ang), jnp.cos(row_ang), jnp.cos(col_ang), jnp.cos(col_ang), zeros], axis=1)
    sin = jnp.concatenate([-jnp.sin(row_ang), jnp.sin(row_ang), -jnp.sin(col_ang), jnp.sin(col_ang), zeros], axis=1)
    return cos, sin


def _gqa_qkv_kernel(x_ref, g_ref, wqt_ref, wk_ref, wvt_ref, qg_ref, kg_ref, cos_ref, sin_ref, cost_ref, sint_ref,
                    qt_ref, k_ref, vt_ref):
    xn = _rms(x_ref[...], g_ref[...]).astype(BF16)
    quarter = HEAD_DIM // 4

    cost = cost_ref[...]
    sint = sint_ref[...]
    qg = qg_ref[...]
    for hh in range(GQA_Q_HEADS):
        y = _dot_nt(wqt_ref[hh * LANES:(hh + 1) * LANES, :], xn)
        ms = jnp.sum(y * y, axis=0, keepdims=True) * (1.0 / HEAD_DIM)
        y = y * lax.rsqrt(ms + EPS) * qg
        partner = jnp.concatenate(
            [y[quarter:2 * quarter], y[:quarter], y[3 * quarter:HEAD_DIM], y[2 * quarter:3 * quarter], y[HEAD_DIM:]],
            axis=0)
        qt_ref[0, hh * LANES:(hh + 1) * LANES, :] = (y * cost + partner * sint).astype(BF16)

    lane = lax.broadcasted_iota(jnp.int32, (1, LANES), 1)
    first = (lane % (2 * quarter)) < quarter
    cos = cos_ref[...]
    sin = sin_ref[...]
    row = lax.broadcasted_iota(jnp.int32, (FA_VROWS, 1), 0)
    for hh in range(GQA_KV_HEADS):
        y = _dot(xn, wk_ref[:, hh * LANES:(hh + 1) * LANES])
        ms = jnp.sum(y * y, axis=-1, keepdims=True) * (1.0 / HEAD_DIM)
        y = y * lax.rsqrt(ms + EPS) * kg_ref[...]
        partner = jnp.where(first, pltpu.roll(y, LANES - quarter, 1), pltpu.roll(y, quarter, 1))
        k_ref[:, hh * LANES:(hh + 1) * LANES] = (y * cos + partner * sin).astype(BF16)
        vt = _dot_nt(wvt_ref[hh * FA_VROWS:(hh + 1) * FA_VROWS, :], xn)
        vt_ref[0, hh * FA_VROWS:(hh + 1) * FA_VROWS, :] = jnp.where(row == HEAD_DIM, 1.0, vt).astype(BF16)


def gqa_qkv(h, g, wq, wk, wv, q_gain, k_gain):
    t, d = h.shape
    tm = FA_TK
    seq_blocks = SEQ // tm
    wqt = _pad_heads(wq, GQA_Q_HEADS, LANES).T.astype(BF16)
    wkp = _pad_heads(wk, GQA_KV_HEADS, LANES).astype(BF16)
    wvt = _pad_heads(wv, GQA_KV_HEADS, FA_VROWS).T.astype(BF16)
    cos, sin = _rope_tables()
    q_scale = (HEAD_DIM ** -0.5) * float(np.log2(np.e))
    cost, sint = (cos * q_scale).T, (sin * q_scale).T
    qg = jnp.pad(q_gain, (0, LANES - HEAD_DIM)).reshape(LANES, 1)
    kg = jnp.pad(k_gain, (0, LANES - HEAD_DIM)).reshape(1, LANES)
    const = lambda shape: pl.BlockSpec(shape, lambda i: (0,) * len(shape))
    return pl.pallas_call(
        _gqa_qkv_kernel,
        grid=(t // tm,),
        in_specs=[
            pl.BlockSpec((tm, d), lambda i: (i, 0)),
            const((1, d)),
            const(wqt.shape),
            const(wkp.shape),
            const(wvt.shape),
            const((LANES, 1)),
            const((1, LANES)),
            pl.BlockSpec((tm, LANES), lambda i: (i % seq_blocks, 0)),
            pl.BlockSpec((tm, LANES), lambda i: (i % seq_blocks, 0)),
            pl.BlockSpec((LANES, tm), lambda i: (0, i % seq_blocks)),
            pl.BlockSpec((LANES, tm), lambda i: (0, i % seq_blocks)),
        ],
        out_specs=[
            pl.BlockSpec((1, GQA_Q_HEADS * LANES, tm), lambda i: (i, 0, 0)),
            pl.BlockSpec((tm, GQA_KV_HEADS * LANES), lambda i: (i, 0)),
            pl.BlockSpec((1, GQA_KV_HEADS * FA_VROWS, tm), lambda i: (i, 0, 0)),
        ],
        out_shape=[
            jax.ShapeDtypeStruct((t // tm, GQA_Q_HEADS * LANES, tm), BF16),
            jax.ShapeDtypeStruct((t, GQA_KV_HEADS * LANES), BF16),
            jax.ShapeDtypeStruct((t // tm, GQA_KV_HEADS * FA_VROWS, tm), BF16),
        ],
        compiler_params=pltpu.CompilerParams(
            dimension_semantics=("parallel",), vmem_limit_bytes=VMEM_LIMIT),
        name="gqa_qkv",
    )(h, g.reshape(1, d), wqt, wkp, wvt, qg, kg, cos, sin, cost, sint)


def _flash_kernel(qt_ref, k_ref, vt_ref, o_ref, m_ref, acc_ref, jump_ref):
    n_kv = SEQ // FA_TK

    def scores(j, g):
        kj = k_ref[pl.ds(pl.multiple_of(j * FA_TK, FA_TK), FA_TK), :]
        return _dot(kj, qt_ref[0, g * LANES:(g + 1) * LANES, :])

    def reset():
        m_ref[...] = jnp.full(m_ref.shape, -jnp.inf, F32)
        acc_ref[...] = jnp.zeros_like(acc_ref)

    def step_exact(j, g, s):
        m_prev = m_ref[g]
        m_new = jnp.maximum(m_prev, jnp.max(s, axis=0, keepdims=True))
        p = jnp.exp2(s - m_new)
        acc_ref[g] = jnp.exp2(m_prev - m_new) * acc_ref[g] + _dot(vt_ref[j], p.astype(BF16))
        m_ref[g] = m_new

    def step_lagged(j, g, s):
        m_prev = m_ref[g]
        p = jnp.exp2(s - m_prev)
        block_max = jnp.max(s, axis=0, keepdims=True)
        m_new = jnp.maximum(m_prev, block_max)
        acc_ref[g] = jnp.exp2(m_prev - m_new) * (acc_ref[g] + _dot(vt_ref[j], p.astype(BF16)))
        m_ref[g] = m_new
        jump_ref[g] = jnp.maximum(jump_ref[g], block_max - m_prev)

    def sweep(j, s, step):
        for g in range(GQA_GROUPS):
            if g + 1 < GQA_GROUPS:
                s_next = scores(j, g + 1)
            else:
                s_next = scores(jnp.minimum(j + 1, n_kv - 1), 0)
            step(j, g, s)
            s = s_next
        return s

    reset()
    jump_ref[...] = jnp.zeros_like(jump_ref)
    s1 = sweep(0, scores(0, 0), step_exact)
    lax.fori_loop(1, n_kv, lambda j, s: sweep(j, s, step_lagged), s1, unroll=FA_UNROLL)

    @pl.when(jnp.max(jump_ref[...]) > FA_MAX_JUMP)
    def _():
        reset()

        def body(j, carry):
            for g in range(GQA_GROUPS):
                step_exact(j, g, scores(j, g))
            return carry

        lax.fori_loop(0, n_kv, body, 0)

    for pair in range(GQA_GROUPS // 2):
        halves = []
        for g in (2 * pair, 2 * pair + 1):
            acc = acc_ref[g]
            halves.append(acc[:HEAD_DIM] / acc[HEAD_DIM:HEAD_DIM + 1])
        o_ref[:, pair * LANES:(pair + 1) * LANES] = jnp.concatenate(halves, axis=0).T.astype(o_ref.dtype)


def flash_attention(qt, k, vt):
    q_blocks = SEQ // FA_TQ
    q_per_tile = FA_TK // FA_TQ
    kv_blocks = SEQ // FA_TK
    gw = GQA_GROUPS * LANES
    return pl.pallas_call(
        _flash_kernel,
        grid=(BATCH, GQA_KV_HEADS, q_blocks),
        in_specs=[
            pl.BlockSpec((1, gw, FA_TQ), lambda b, kh, i: (b * kv_blocks + i // q_per_tile, kh, i % q_per_tile)),
            pl.BlockSpec((SEQ, LANES), lambda b, kh, i: (b, kh)),
            pl.BlockSpec((kv_blocks, FA_VROWS, FA_TK), lambda b, kh, i: (b, kh, 0)),
        ],
        out_specs=pl.BlockSpec((FA_TQ, GQA_GROUPS * HEAD_DIM), lambda b, kh, i: (b * q_blocks + i, kh)),
        out_shape=jax.ShapeDtypeStruct((TOKENS, D_MODEL), BF16),
        scratch_shapes=[
            pltpu.VMEM((GQA_GROUPS, 1, FA_TQ), F32),
            pltpu.VMEM((GQA_GROUPS, FA_VROWS, FA_TQ), F32),
            pltpu.VMEM((GQA_GROUPS, 1, FA_TQ), F32),
        ],
        compiler_params=pltpu.CompilerParams(
            dimension_semantics=("parallel", "parallel", "parallel"), vmem_limit_bytes=VMEM_LIMIT),
        name="gqa_flash",
    )(qt, k, vt)


def _router_kernel(h_ref, gin_ref, wr_ref, xn_ref, gates_ref, rank_ref, rankt_ref, count_ref):
    tm = h_ref.shape[0]
    lane = lax.broadcasted_iota(jnp.int32, (1, LANES), 1)
    x = _rms(h_ref[...], gin_ref[...])
    xn_ref[...] = x.astype(BF16)
    logits = jnp.dot(x, wr_ref[...], preferred_element_type=F32, precision=lax.Precision.HIGHEST)
    logits = jnp.where(lane < N_EXPERTS, logits, -jnp.inf)
    m1 = jnp.max(logits, axis=-1, keepdims=True)
    i1 = jnp.min(jnp.where(logits == m1, lane, LANES), axis=-1, keepdims=True)
    rest = jnp.where(lane == i1, -jnp.inf, logits)
    m2 = jnp.max(rest, axis=-1, keepdims=True)
    i2 = jnp.min(jnp.where(rest == m2, lane, LANES), axis=-1, keepdims=True)
    e2 = jnp.exp(m2 - m1)
    denom = 1.0 + e2
    gates_ref[...] = jnp.where(lane == i1, 1.0 / denom, 0.0) + jnp.where(lane == i2, e2 / denom, 0.0)
    chosen = jnp.logical_or(lane == i1, lane == i2)
    earlier = lax.broadcasted_iota(jnp.int32, (tm, tm), 1) < lax.broadcasted_iota(jnp.int32, (tm, tm), 0)
    prefix = _dot(jnp.where(earlier, 1.0, 0.0).astype(BF16), jnp.where(chosen, 1.0, 0.0).astype(BF16))
    rank = jnp.where(chosen, prefix, -1.0)
    rank_ref[...] = rank
    rankt_ref[0] = rank.T
    count_ref[0] = jnp.sum(jnp.where(chosen, 1.0, 0.0), axis=0, keepdims=True)


def _moe_kernel(ng_ref, xn_ref, gates_ref, rank_ref, rankt_ref, wg_ref, wu_ref, wd_ref, h_ref, gout_ref, o_ref,
                xs_ref, yacc_ref, acc_ref):
    i = pl.program_id(0)
    e = pl.program_id(1)
    j = pl.program_id(2)
    last_e = pl.num_programs(1) - 1
    last_j = pl.num_programs(2) - 1
    tm = xn_ref.shape[0]
    n_groups = ng_ref[i * N_EXPERTS + e]

    @pl.when(jnp.logical_and(e == 0, j == 0))
    def _():
        acc_ref[...] = jnp.zeros_like(acc_ref)

    @pl.when(j == 0)
    def _():
        rank_row = rankt_ref[0, pl.ds(e, 1), :]

        def compact(c, carry):
            slot = (lax.broadcasted_iota(jnp.int32, (MOE_GROUP, 1), 0) + c * MOE_GROUP).astype(F32)
            onehot = jnp.where(rank_row == slot, 1.0, 0.0).astype(BF16)
            xs_ref[c] = _dot(onehot, xn_ref[...]).astype(BF16)
            yacc_ref[c] = jnp.zeros(yacc_ref.shape[1:], F32)
            return carry

        lax.fori_loop(0, n_groups, compact, 0)

    def expert(c, carry):
        xs = xs_ref[c]
        gate = _dot(xs, wg_ref[0])
        up = _dot(xs, wu_ref[0])
        act = (gate * jax.nn.sigmoid(gate) * up).astype(BF16)
        yacc_ref[c] += _dot(act, wd_ref[0])
        return carry

    lax.fori_loop(0, n_groups, expert, 0)

    @pl.when(j == last_j)
    def _():
        lane = lax.broadcasted_iota(jnp.int32, (1, LANES), 1)
        rank_col = jnp.sum(jnp.where(lane == e, rank_ref[...], 0.0), axis=-1, keepdims=True)
        gate_col = jnp.sum(jnp.where(lane == e, gates_ref[...], 0.0), axis=-1, keepdims=True)

        def expand(c, carry):
            slot = (lax.broadcasted_iota(jnp.int32, (1, MOE_GROUP), 1) + c * MOE_GROUP).astype(F32)
            onehot = jnp.where(rank_col == slot, 1.0, 0.0).astype(BF16)
            acc_ref[...] += gate_col * _dot(onehot, yacc_ref[c].astype(BF16))
            return carry

        lax.fori_loop(0, n_groups, expand, 0)

    @pl.when(jnp.logical_and(e == last_e, j == last_j))
    def _():
        o_ref[...] = h_ref[...] + _rms(acc_ref[...], gout_ref[...])


def moe_block(h, g_in, w_router, wg, wu, wd, g_out, tm=MOE_TM, tf=MOE_TF):
    t, d = h.shape
    n_e, _, f = wg.shape
    n_tiles = t // tm
    wr = jnp.pad(w_router, ((0, 0), (0, LANES - n_e)))
    xn, gates, rank, rankt, counts = pl.pallas_call(
        _router_kernel,
        grid=(n_tiles,),
        in_specs=[
            pl.BlockSpec((tm, d), lambda i: (i, 0)),
            pl.BlockSpec((1, d), lambda i: (0, 0)),
            pl.BlockSpec((d, LANES), lambda i: (0, 0)),
        ],
        out_specs=[
            pl.BlockSpec((tm, d), lambda i: (i, 0)),
            pl.BlockSpec((tm, LANES), lambda i: (i, 0)),
            pl.BlockSpec((tm, LANES), lambda i: (i, 0)),
            pl.BlockSpec((1, LANES, tm), lambda i: (i, 0, 0)),
            pl.BlockSpec((1, 1, LANES), lambda i: (i, 0, 0)),
        ],
        out_shape=[
            jax.ShapeDtypeStruct((t, d), BF16),
            jax.ShapeDtypeStruct((t, LANES), F32),
            jax.ShapeDtypeStruct((t, LANES), F32),
            jax.ShapeDtypeStruct((n_tiles, LANES, tm), F32),
            jax.ShapeDtypeStruct((n_tiles, 1, LANES), F32),
        ],
        compiler_params=pltpu.CompilerParams(
            dimension_semantics=("parallel",), vmem_limit_bytes=VMEM_LIMIT),
        name="moe_router",
    )(h, g_in.reshape(1, d), wr)
    n_groups = (counts[:, 0, :n_e].astype(jnp.int32) + (MOE_GROUP - 1)) // MOE_GROUP
    max_groups = -(-tm // MOE_GROUP)
    grid_spec = pltpu.PrefetchScalarGridSpec(
        num_scalar_prefetch=1,
        grid=(n_tiles, n_e, f // tf),
        in_specs=[
            pl.BlockSpec((tm, d), lambda i, e, j, ng: (i, 0)),
            pl.BlockSpec((tm, LANES), lambda i, e, j, ng: (i, 0)),
            pl.BlockSpec((tm, LANES), lambda i, e, j, ng: (i, 0)),
            pl.BlockSpec((1, LANES, tm), lambda i, e, j, ng: (i, 0, 0)),
            pl.BlockSpec((1, d, tf), lambda i, e, j, ng: (e, 0, j)),
            pl.BlockSpec((1, d, tf), lambda i, e, j, ng: (e, 0, j)),
            pl.BlockSpec((1, tf, d), lambda i, e, j, ng: (e, j, 0)),
            pl.BlockSpec((tm, d), lambda i, e, j, ng: (i, 0)),
            pl.BlockSpec((1, d), lambda i, e, j, ng: (0, 0)),
        ],
        out_specs=pl.BlockSpec((tm, d), lambda i, e, j, ng: (i, 0)),
        scratch_shapes=[
            pltpu.VMEM((max_groups, MOE_GROUP, d), BF16),
            pltpu.VMEM((max_groups, MOE_GROUP, d), F32),
            pltpu.VMEM((tm, d), F32),
        ],
    )
    return pl.pallas_call(
        _moe_kernel,
        grid_spec=grid_spec,
        out_shape=jax.ShapeDtypeStruct((t, d), F32),
        compiler_params=pltpu.CompilerParams(
            dimension_semantics=("parallel", "arbitrary", "arbitrary"), vmem_limit_bytes=VMEM_LIMIT),
        name="moe_swiglu",
    )(n_groups.reshape(-1), xn, gates, rank, rankt, wg, wu, wd, h, g_out.reshape(1, d))


def kernel(x, norm_g, na_w_qkv, na_rpb, na_w_o, gqa_w_qkv, gqa_q_norm, gqa_k_norm, gqa_w_o,
           ffn_w_gate, ffn_w_up, ffn_w_down, moe_w_router, moe_w_gate, moe_w_up, moe_w_down):
    assert x.shape == (BATCH, SEQ, D_MODEL)
    h = x.reshape(TOKENS, D_MODEL)

    w_qkv = na_w_qkv[0].at[:, :D_MODEL].multiply(HEAD_DIM ** -0.5).astype(BF16)
    qkv = norm_matmul(h, norm_g[0, 0], w_qkv)
    a = neighborhood_attention(qkv, _na_bias_tables(na_rpb[0]))
    h = proj_norm_res(a, na_w_o[0].astype(BF16), norm_g[0, 1], h)
    h = ffn_block(h, norm_g[0, 2], ffn_w_gate[0].astype(BF16), ffn_w_up[0].astype(BF16),
                  ffn_w_down[0].astype(BF16), norm_g[0, 3])

    nq = GQA_Q_HEADS * HEAD_DIM
    nkv = GQA_KV_HEADS * HEAD_DIM
    wq, wk, wv = gqa_w_qkv[0][:, :nq], gqa_w_qkv[0][:, nq:nq + nkv], gqa_w_qkv[0][:, nq + nkv:]
    qt, k, vt = gqa_qkv(h, norm_g[1, 0], wq, wk, wv, gqa_q_norm[0], gqa_k_norm[0])
    o = flash_attention(qt, k, vt)
    h = proj_norm_res(o, gqa_w_o[0].astype(BF16), norm_g[1, 1], h)
    h = moe_block(h, norm_g[1, 2], moe_w_router[0], moe_w_gate[0].astype(BF16), moe_w_up[0].astype(BF16),
                  moe_w_down[0].astype(BF16), norm_g[1, 3])
    return h.reshape(BATCH, SEQ, D_MODEL)
```

```python
import functools

import jax
import jax.numpy as jnp
import numpy as np
from jax import lax
from jax.experimental import pallas as pl
from jax.experimental.pallas import tpu as pltpu

F32 = jnp.float32
BF16 = jnp.bfloat16

D_MODEL = 1024
BATCH = 2
SEQ = 8192
TOKENS = BATCH * SEQ
GRID_W = 64
GRID_H = SEQ // GRID_W
NA_HEADS = 16
HEAD_DIM = 64
WIN_H = 8
WIN_W = 16
GQA_Q_HEADS = 16
GQA_KV_HEADS = 4
GQA_GROUPS = GQA_Q_HEADS // GQA_KV_HEADS
ROPE_THETA = 10000.0
D_FF = 3584
N_EXPERTS = 8
EPS = 1e-6
LANES = 128
NEG_BIG = -1e30

VMEM_LIMIT = 56 * 1024 * 1024

NA_QROWS = 4
NA_QTOK = NA_QROWS * GRID_W
NA_BAND_BLOCKS = 3
NA_KTOK = NA_BAND_BLOCKS * NA_QTOK
NA_ROW_BLOCKS = GRID_H // NA_QROWS
NA_HEADS_PER_STEP = 4

FA_TQ = 512
FA_TK = 512
MOE_TM = 1024
MOE_TF = 896
MOE_GROUP = 384
MOE_GROUP_SMALL = 256

FA_UNROLL = 3
FA_MAX_JUMP = 24.0
FA_VROWS = 80


def _rms(x, g):
    ms = jnp.mean(x * x, axis=-1, keepdims=True)
    return x * lax.rsqrt(ms + EPS) * g


def _dot(a, b):
    return jnp.dot(a, b, preferred_element_type=F32)


def _dot_nt(a, b):
    return lax.dot_general(a, b, (((1,), (1,)), ((), ())), preferred_element_type=F32)


def _norm_matmul_kernel(x_ref, g_ref, w_ref, scale_ref, o_ref, *, n_chunk):
    xn = _rms(x_ref[...], g_ref[...]).astype(BF16)
    n = o_ref.shape[1]
    for n0 in range(0, n, n_chunk):
        y = _dot(xn, w_ref[:, n0:n0 + n_chunk]) * scale_ref[:, n0:n0 + n_chunk]
        o_ref[:, n0:n0 + n_chunk] = y.astype(o_ref.dtype)


def norm_matmul(x, g, w, col_scale, tm=512, n_chunk=512):
    t, d = x.shape
    n = w.shape[1]
    return pl.pallas_call(
        functools.partial(_norm_matmul_kernel, n_chunk=n_chunk),
        grid=(t // tm,),
        in_specs=[
            pl.BlockSpec((tm, d), lambda i: (i, 0)),
            pl.BlockSpec((1, d), lambda i: (0, 0)),
            pl.BlockSpec((d, n), lambda i: (0, 0)),
            pl.BlockSpec((1, n), lambda i: (0, 0)),
        ],
        out_specs=pl.BlockSpec((tm, n), lambda i: (i, 0)),
        out_shape=jax.ShapeDtypeStruct((t, n), BF16),
        compiler_params=pltpu.CompilerParams(
            dimension_semantics=("parallel",), vmem_limit_bytes=VMEM_LIMIT),
        name="norm_matmul",
    )(x, g.reshape(1, d), w, col_scale.reshape(1, n))


def _proj_norm_res_kernel(a_ref, w_ref, g_ref, h_ref, o_ref):
    y = _dot(a_ref[...], w_ref[...])
    o_ref[...] = h_ref[...] + _rms(y, g_ref[...])


def proj_norm_res(a, w, g, h, tm=512):
    t, k = a.shape
    d = w.shape[1]
    return pl.pallas_call(
        _proj_norm_res_kernel,
        grid=(t // tm,),
        in_specs=[
            pl.BlockSpec((tm, k), lambda i: (i, 0)),
            pl.BlockSpec((k, d), lambda i: (0, 0)),
            pl.BlockSpec((1, d), lambda i: (0, 0)),
            pl.BlockSpec((tm, d), lambda i: (i, 0)),
        ],
        out_specs=pl.BlockSpec((tm, d), lambda i: (i, 0)),
        out_shape=jax.ShapeDtypeStruct((t, d), F32),
        compiler_params=pltpu.CompilerParams(
            dimension_semantics=("parallel",), vmem_limit_bytes=VMEM_LIMIT),
        name="proj_norm_res",
    )(a, w, g.reshape(1, d), h)


def _na_bias_tables(rpb):
    i = np.arange(NA_QROWS)[:, None]
    m = np.arange(NA_BAND_BLOCKS * NA_QROWS)[None, :]
    ridx, rvalid = [], []
    for rb in (0, 1, NA_ROW_BLOCKS - 1):
        band0 = NA_QROWS * min(max(rb - 1, 0), NA_ROW_BLOCKS - NA_BAND_BLOCKS)
        r = NA_QROWS * rb + i
        kr = band0 + m
        rs = np.clip(r - WIN_H // 2, 0, GRID_H - WIN_H)
        rvalid.append((kr >= rs) & (kr < rs + WIN_H))
        ridx.append(np.clip(kr - r + WIN_H - 1, 0, 2 * WIN_H - 2))
    n_dr = 2 * WIN_H - 1
    plane = np.where(np.stack(rvalid), np.stack(ridx), n_dr)
    c = np.arange(GRID_W)[:, None]
    kc = np.arange(GRID_W)[None, :]
    cs = np.clip(c - WIN_W // 2, 0, GRID_W - WIN_W)
    cvalid = (kc >= cs) & (kc < cs + WIN_W)
    left = GRID_W - WIN_W
    padded = jnp.pad(rpb, ((0, 0), (0, 0), (left, left)))
    cols = jnp.stack([padded[:, :, GRID_W - 1 - cq:2 * GRID_W - 1 - cq] for cq in range(GRID_W)], axis=2)
    cols = jnp.where(cvalid[None, None], cols, NEG_BIG)
    cols = jnp.concatenate([cols, jnp.full_like(cols[:, :1], NEG_BIG)], axis=1)
    tab = jnp.stack([
        jnp.stack([jnp.stack([cols[:, u] for u in plane[case, i]], axis=2) for i in range(NA_QROWS)], axis=1)
        for case in range(plane.shape[0])], axis=0)
    return tab.reshape(plane.shape[0], NA_HEADS, NA_QTOK, NA_KTOK)


def _na_kernel(q_ref, k0_ref, k1_ref, k2_ref, v0_ref, v1_ref, v2_ref, bias_ref, o_ref):
    lane = lax.broadcasted_iota(jnp.int32, (1, LANES), 1)
    low = lane < HEAD_DIM

    def scores(head):
        sl = slice((head // 2) * LANES, (head // 2 + 1) * LANES)
        q = q_ref[:, sl]
        k = jnp.concatenate([k0_ref[:, sl], k1_ref[:, sl], k2_ref[:, sl]], axis=0)
        keep = low if head % 2 == 0 else jnp.logical_not(low)
        return _dot_nt(jnp.where(keep, q, jnp.zeros_like(q)), k)

    outs = []
    s = scores(0)
    for head in range(NA_HEADS_PER_STEP):
        s_next = scores(head + 1) if head + 1 < NA_HEADS_PER_STEP else None
        sl = slice((head // 2) * LANES, (head // 2 + 1) * LANES)
        v = jnp.concatenate([v0_ref[:, sl], v1_ref[:, sl], v2_ref[:, sl]], axis=0)
        keep = low if head % 2 == 0 else jnp.logical_not(low)
        v = jnp.where(keep, v, jnp.ones_like(v))
        s = s + bias_ref[0, head]
        p = jnp.exp2(s - jnp.max(s, axis=-1, keepdims=True))
        pv = _dot(p.astype(BF16), v)
        outs.append(pv / pltpu.roll(pv, HEAD_DIM, 1))
        if head % 2 == 1:
            o_ref[:, sl] = jnp.where(low, outs[head - 1], outs[head]).astype(o_ref.dtype)
        s = s_next


def neighborhood_attention(qkv, bias_tab):
    hw = NA_HEADS_PER_STEP * HEAD_DIM
    nh = D_MODEL // hw
    blocks_per_seq = SEQ // NA_QTOK

    def band(rb):
        return jnp.clip(rb - 1, 0, NA_ROW_BLOCKS - NA_BAND_BLOCKS)

    def case(rb):
        return jnp.where(rb == 0, 0, jnp.where(rb == NA_ROW_BLOCKS - 1, 2, 1))

    q_spec = pl.BlockSpec((NA_QTOK, hw), lambda h, b, rb: (b * blocks_per_seq + rb, h))
    kv_specs = [
        pl.BlockSpec((NA_QTOK, hw),
                     lambda h, b, rb, j=j, part=part: (b * blocks_per_seq + band(rb) + j, part * nh + h))
        for part in (1, 2) for j in range(NA_BAND_BLOCKS)
    ]
    bias_spec = pl.BlockSpec((1, NA_HEADS_PER_STEP, NA_QTOK, NA_KTOK),
                             lambda h, b, rb: (case(rb), h, 0, 0))
    return pl.pallas_call(
        _na_kernel,
        grid=(nh, BATCH, NA_ROW_BLOCKS),
        in_specs=[q_spec] + kv_specs + [bias_spec],
        out_specs=pl.BlockSpec((NA_QTOK, hw), lambda h, b, rb: (b * blocks_per_seq + rb, h)),
        out_shape=jax.ShapeDtypeStruct((TOKENS, D_MODEL), BF16),
        compiler_params=pltpu.CompilerParams(
            dimension_semantics=("parallel", "parallel", "parallel"), vmem_limit_bytes=VMEM_LIMIT),
        name="na_attention",
    )(qkv, qkv, qkv, qkv, qkv, qkv, qkv, bias_tab)


def _ffn_kernel(h_ref, gin_ref, wg_ref, wu_ref, wd_ref, gout_ref, o_ref, xn_ref, acc_ref):
    j = pl.program_id(1)

    @pl.when(j == 0)
    def _():
        xn_ref[...] = _rms(h_ref[...], gin_ref[...]).astype(BF16)
        acc_ref[...] = jnp.zeros_like(acc_ref)

    xn = xn_ref[...]
    gate = _dot(xn, wg_ref[...])
    up = _dot(xn, wu_ref[...])
    act = (gate * jax.nn.sigmoid(gate) * up).astype(BF16)
    acc_ref[...] += _dot(act, wd_ref[...])

    @pl.when(j == pl.num_programs(1) - 1)
    def _():
        o_ref[...] = h_ref[...] + _rms(acc_ref[...], gout_ref[...])


def ffn_block(h, g_in, wg, wu, wd, g_out, tm=1024, tf=512):
    t, d = h.shape
    f = wg.shape[1]
    return pl.pallas_call(
        _ffn_kernel,
        grid=(t // tm, f // tf),
        in_specs=[
            pl.BlockSpec((tm, d), lambda i, j: (i, 0)),
            pl.BlockSpec((1, d), lambda i, j: (0, 0)),
            pl.BlockSpec((d, tf), lambda i, j: (0, j)),
            pl.BlockSpec((d, tf), lambda i, j: (0, j)),
            pl.BlockSpec((tf, d), lambda i, j: (j, 0)),
            pl.BlockSpec((1, d), lambda i, j: (0, 0)),
        ],
        out_specs=pl.BlockSpec((tm, d), lambda i, j: (i, 0)),
        out_shape=jax.ShapeDtypeStruct((t, d), F32),
        scratch_shapes=[pltpu.VMEM((tm, d), BF16), pltpu.VMEM((tm, d), F32)],
        compiler_params=pltpu.CompilerParams(
            dimension_semantics=("parallel", "arbitrary"), vmem_limit_bytes=VMEM_LIMIT),
        name="ffn_swiglu",
    )(h, g_in.reshape(1, d), wg, wu, wd, g_out.reshape(1, d))


def _pad_heads(w, n_heads, width):
    d = w.shape[0]
    w = w.reshape(d, n_heads, HEAD_DIM)
    return jnp.pad(w, ((0, 0), (0, 0), (0, width - HEAD_DIM))).reshape(d, n_heads * width)


def _rope_tables():
    quarter = HEAD_DIM // 4
    freqs = ROPE_THETA ** (-jnp.arange(quarter, dtype=F32) / quarter)
    t = jnp.arange(SEQ)
    row_ang = (t // GRID_W).astype(F32)[:, None] * freqs[None, :]
    col_ang = (t % GRID_W).astype(F32)[:, None] * freqs[None, :]
    zeros = jnp.zeros((SEQ, LANES - HEAD_DIM), F32)
    cos = jnp.concatenate([jnp.cos(row_ang), jnp.cos(row_ang), jnp.cos(col_ang), jnp.cos(col_ang), zeros], axis=1)
    sin = jnp.concatenate([-jnp.sin(row_ang), jnp.sin(row_ang), -jnp.sin(col_ang), jnp.sin(col_ang), zeros], axis=1)
    return cos, sin


def _gqa_qkv_kernel(x_ref, g_ref, wq_ref, wk_ref, wv_ref, qg_ref, kg_ref, cos_ref, sin_ref, cost_ref, sint_ref,
                    qt_ref, k_ref, vt_ref):
    xn = _rms(x_ref[...], g_ref[...]).astype(BF16)
    tm = xn.shape[0]
    quarter = HEAD_DIM // 4

    cost = cost_ref[...]
    sint = sint_ref[...]
    qg = qg_ref[...]
    zero_rows = jnp.zeros((LANES - HEAD_DIM, tm), BF16)
    wide = 2 * LANES
    for pair in range(GQA_Q_HEADS // 2):
        if pair % 2 == 0:
            y_wide = _dot(xn, wq_ref[:, pair * LANES:pair * LANES + wide])
        yt = y_wide[:, (pair % 2) * LANES:(pair % 2 + 1) * LANES].T
        for half in range(2):
            y = yt[half * HEAD_DIM:(half + 1) * HEAD_DIM]
            ms = jnp.sum(y * y, axis=0, keepdims=True) * (1.0 / HEAD_DIM)
            y = y * lax.rsqrt(ms + EPS) * qg
            partner = jnp.concatenate(
                [y[quarter:2 * quarter], y[:quarter], y[3 * quarter:], y[2 * quarter:3 * quarter]], axis=0)
            base = (2 * pair + half) * LANES
            qt_ref[0, base:base + HEAD_DIM, :] = (y * cost + partner * sint).astype(BF16)
            qt_ref[0, base + HEAD_DIM:base + LANES, :] = zero_rows

    lane = lax.broadcasted_iota(jnp.int32, (1, LANES), 1)
    first = (lane % (2 * quarter)) < quarter
    cos = cos_ref[...]
    sin = sin_ref[...]
    for hh in range(GQA_KV_HEADS):
        if hh % 2 == 0:
            y_wide = _dot(xn, wk_ref[:, hh * LANES:hh * LANES + wide])
        y = y_wide[:, (hh % 2) * LANES:(hh % 2 + 1) * LANES]
        ms = jnp.sum(y * y, axis=-1, keepdims=True) * (1.0 / HEAD_DIM)
        y = y * lax.rsqrt(ms + EPS) * kg_ref[...]
        partner = jnp.where(first, pltpu.roll(y, LANES - quarter, 1), pltpu.roll(y, quarter, 1))
        k_ref[:, hh * LANES:(hh + 1) * LANES] = (y * cos + partner * sin).astype(BF16)

    fill = FA_VROWS - HEAD_DIM
    ones_then_zeros = jnp.where(lax.broadcasted_iota(jnp.int32, (fill, tm), 0) == 0, 1.0, 0.0).astype(BF16)
    assert GQA_KV_HEADS * HEAD_DIM == wide
    v_wide = _dot(xn, wv_ref[...])
    for pair in range(GQA_KV_HEADS // 2):
        vt = v_wide[:, pair * LANES:(pair + 1) * LANES].T
        for half in range(2):
            base = (2 * pair + half) * FA_VROWS
            vt_ref[0, base:base + HEAD_DIM, :] = vt[half * HEAD_DIM:(half + 1) * HEAD_DIM].astype(BF16)
            vt_ref[0, base + HEAD_DIM:base + FA_VROWS, :] = ones_then_zeros


def gqa_qkv(h, g, wq, wk, wv, q_gain, k_gain):
    t, d = h.shape
    tm = FA_TK
    seq_blocks = SEQ // tm
    wqb = wq.astype(BF16)
    wkp = _pad_heads(wk, GQA_KV_HEADS, LANES).astype(BF16)
    wvb = wv.astype(BF16)
    cos, sin = _rope_tables()
    q_scale = (HEAD_DIM ** -0.5) * float(np.log2(np.e))
    cost, sint = (cos[:, :HEAD_DIM] * q_scale).T, (sin[:, :HEAD_DIM] * q_scale).T
    qg = q_gain.reshape(HEAD_DIM, 1)
    kg = jnp.pad(k_gain, (0, LANES - HEAD_DIM)).reshape(1, LANES)
    const = lambda shape: pl.BlockSpec(shape, lambda i: (0,) * len(shape))
    return pl.pallas_call(
        _gqa_qkv_kernel,
        grid=(t // tm,),
        in_specs=[
            pl.BlockSpec((tm, d), lambda i: (i, 0)),
            const((1, d)),
            const(wqb.shape),
            const(wkp.shape),
            const(wvb.shape),
            const((HEAD_DIM, 1)),
            const((1, LANES)),
            pl.BlockSpec((tm, LANES), lambda i: (i % seq_blocks, 0)),
            pl.BlockSpec((tm, LANES), lambda i: (i % seq_blocks, 0)),
            pl.BlockSpec((HEAD_DIM, tm), lambda i: (0, i % seq_blocks)),
            pl.BlockSpec((HEAD_DIM, tm), lambda i: (0, i % seq_blocks)),
        ],
        out_specs=[
            pl.BlockSpec((1, GQA_Q_HEADS * LANES, tm), lambda i: (i, 0, 0)),
            pl.BlockSpec((tm, GQA_KV_HEADS * LANES), lambda i: (i, 0)),
            pl.BlockSpec((1, GQA_KV_HEADS * FA_VROWS, tm), lambda i: (i, 0, 0)),
        ],
        out_shape=[
            jax.ShapeDtypeStruct((t // tm, GQA_Q_HEADS * LANES, tm), BF16),
            jax.ShapeDtypeStruct((t, GQA_KV_HEADS * LANES), BF16),
            jax.ShapeDtypeStruct((t // tm, GQA_KV_HEADS * FA_VROWS, tm), BF16),
        ],
        compiler_params=pltpu.CompilerParams(
            dimension_semantics=("parallel",), vmem_limit_bytes=VMEM_LIMIT),
        name="gqa_qkv",
    )(h, g.reshape(1, d), wqb, wkp, wvb, qg, kg, cos, sin, cost, sint)


def _flash_kernel(qt_ref, k_ref, vt_ref, o_ref, m_ref, acc_ref, jump_ref):
    n_kv = SEQ // FA_TK

    def scores(j, g):
        kj = k_ref[pl.ds(pl.multiple_of(j * FA_TK, FA_TK), FA_TK), :]
        return _dot(kj, qt_ref[0, g * LANES:(g + 1) * LANES, :])

    def reset():
        m_ref[...] = jnp.full(m_ref.shape, -jnp.inf, F32)
        acc_ref[...] = jnp.zeros_like(acc_ref)

    def step_exact(j, g, s):
        m_prev = m_ref[g]
        m_new = jnp.maximum(m_prev, jnp.max(s, axis=0, keepdims=True))
        p = jnp.exp2(s - m_new)
        acc_ref[g] = jnp.exp2(m_prev - m_new) * acc_ref[g] + _dot(vt_ref[j], p.astype(BF16))
        m_ref[g] = m_new

    def step_lagged(j, g, s):
        m_prev = m_ref[g]
        p = jnp.exp2(s - m_prev)
        block_max = jnp.max(s, axis=0, keepdims=True)
        m_new = jnp.maximum(m_prev, block_max)
        acc_ref[g] = jnp.exp2(m_prev - m_new) * (acc_ref[g] + _dot(vt_ref[j], p.astype(BF16)))
        m_ref[g] = m_new
        jump_ref[g] = jnp.maximum(jump_ref[g], block_max - m_prev)

    def sweep(j, s, step):
        for g in range(GQA_GROUPS):
            if g + 1 < GQA_GROUPS:
                s_next = scores(j, g + 1)
            else:
                s_next = scores(jnp.minimum(j + 1, n_kv - 1), 0)
            step(j, g, s)
            s = s_next
        return s

    reset()
    jump_ref[...] = jnp.zeros_like(jump_ref)
    s1 = sweep(0, scores(0, 0), step_exact)
    lax.fori_loop(1, n_kv, lambda j, s: sweep(j, s, step_lagged), s1, unroll=FA_UNROLL)

    @pl.when(jnp.max(jump_ref[...]) > FA_MAX_JUMP)
    def _():
        reset()

        def body(j, carry):
            for g in range(GQA_GROUPS):
                step_exact(j, g, scores(j, g))
            return carry

        lax.fori_loop(0, n_kv, body, 0)

    for pair in range(GQA_GROUPS // 2):
        halves = []
        for g in (2 * pair, 2 * pair + 1):
            acc = acc_ref[g]
            halves.append(acc[:HEAD_DIM] / acc[HEAD_DIM:HEAD_DIM + 1])
        o_ref[:, pair * LANES:(pair + 1) * LANES] = jnp.concatenate(halves, axis=0).T.astype(o_ref.dtype)


def flash_attention(qt, k, vt):
    q_blocks = SEQ // FA_TQ
    q_per_tile = FA_TK // FA_TQ
    kv_blocks = SEQ // FA_TK
    gw = GQA_GROUPS * LANES
    return pl.pallas_call(
        _flash_kernel,
        grid=(BATCH, GQA_KV_HEADS, q_blocks),
        in_specs=[
            pl.BlockSpec((1, gw, FA_TQ), lambda b, kh, i: (b * kv_blocks + i // q_per_tile, kh, i % q_per_tile)),
            pl.BlockSpec((SEQ, LANES), lambda b, kh, i: (b, kh)),
            pl.BlockSpec((kv_blocks, FA_VROWS, FA_TK), lambda b, kh, i: (b, kh, 0)),
        ],
        out_specs=pl.BlockSpec((FA_TQ, GQA_GROUPS * HEAD_DIM), lambda b, kh, i: (b * q_blocks + i, kh)),
        out_shape=jax.ShapeDtypeStruct((TOKENS, D_MODEL), BF16),
        scratch_shapes=[
            pltpu.VMEM((GQA_GROUPS, 1, FA_TQ), F32),
            pltpu.VMEM((GQA_GROUPS, FA_VROWS, FA_TQ), F32),
            pltpu.VMEM((GQA_GROUPS, 1, FA_TQ), F32),
        ],
        compiler_params=pltpu.CompilerParams(
            dimension_semantics=("parallel", "parallel", "parallel"), vmem_limit_bytes=VMEM_LIMIT),
        name="gqa_flash",
    )(qt, k, vt)


def _router_kernel(h_ref, gin_ref, wr_ref, xn_ref, gates_ref, rank_ref, rankt_ref, count_ref):
    tm = h_ref.shape[0]
    lane = lax.broadcasted_iota(jnp.int32, (1, LANES), 1)
    x = _rms(h_ref[...], gin_ref[...])
    xn_ref[...] = x.astype(BF16)
    logits = jnp.dot(x, wr_ref[...], preferred_element_type=F32, precision=lax.Precision.HIGHEST)
    logits = jnp.where(lane < N_EXPERTS, logits, -jnp.inf)
    m1 = jnp.max(logits, axis=-1, keepdims=True)
    i1 = jnp.min(jnp.where(logits == m1, lane, LANES), axis=-1, keepdims=True)
    rest = jnp.where(lane == i1, -jnp.inf, logits)
    m2 = jnp.max(rest, axis=-1, keepdims=True)
    i2 = jnp.min(jnp.where(rest == m2, lane, LANES), axis=-1, keepdims=True)
    e2 = jnp.exp(m2 - m1)
    denom = 1.0 + e2
    gates_ref[...] = jnp.where(lane == i1, 1.0 / denom, 0.0) + jnp.where(lane == i2, e2 / denom, 0.0)
    chosen = jnp.logical_or(lane == i1, lane == i2)
    earlier = lax.broadcasted_iota(jnp.int32, (tm, tm), 1) < lax.broadcasted_iota(jnp.int32, (tm, tm), 0)
    prefix = _dot(jnp.where(earlier, 1.0, 0.0).astype(BF16), jnp.where(chosen, 1.0, 0.0).astype(BF16))
    rank = jnp.where(chosen, prefix, -1.0)
    rank_ref[...] = rank
    rankt_ref[0] = rank.T
    count_ref[0] = jnp.sum(jnp.where(chosen, 1.0, 0.0), axis=0, keepdims=True)


def _moe_kernel(cnt_ref, xn_ref, gates_ref, rank_ref, rankt_ref, wg_ref, wu_ref, wd_ref, h_ref, gout_ref, o_ref,
                xs_ref, yacc_ref, acc_ref):
    i = pl.program_id(0)
    e = pl.program_id(1)
    j = pl.program_id(2)
    last_e = pl.num_programs(1) - 1
    last_j = pl.num_programs(2) - 1
    n_rows = cnt_ref[i * N_EXPERTS + e]

    @pl.when(jnp.logical_and(e == 0, j == 0))
    def _():
        acc_ref[...] = jnp.zeros_like(acc_ref)

    def run(group, n_groups):
        def over_groups(fn):
            if isinstance(n_groups, int):
                for c in range(n_groups):
                    fn(c * group)
            else:
                def body(c, carry):
                    fn(pl.multiple_of(c * group, LANES))
                    return carry

                lax.fori_loop(0, n_groups, body, 0)

        @pl.when(j == 0)
        def _():
            rank_row = rankt_ref[0, pl.ds(e, 1), :]

            def compact(r0):
                slot = (lax.broadcasted_iota(jnp.int32, (group, 1), 0) + r0).astype(F32)
                onehot = jnp.where(rank_row == slot, 1.0, 0.0).astype(BF16)
                xs_ref[pl.ds(r0, group), :] = _dot(onehot, xn_ref[...]).astype(BF16)
                yacc_ref[pl.ds(r0, group), :] = jnp.zeros((group, yacc_ref.shape[1]), F32)

            over_groups(compact)

        def expert(r0):
            xs = xs_ref[pl.ds(r0, group), :]
            gate = _dot(xs, wg_ref[0])
            up = _dot(xs, wu_ref[0])
            act = (gate * jax.nn.sigmoid(gate) * up).astype(BF16)
            yacc_ref[pl.ds(r0, group), :] += _dot(act, wd_ref[0])

        over_groups(expert)

        @pl.when(j == last_j)
        def _():
            lane = lax.broadcasted_iota(jnp.int32, (1, LANES), 1)
            rank_col = jnp.sum(jnp.where(lane == e, rank_ref[...], 0.0), axis=-1, keepdims=True)
            gate_col = jnp.sum(jnp.where(lane == e, gates_ref[...], 0.0), axis=-1, keepdims=True)

            def expand(r0):
                slot = (lax.broadcasted_iota(jnp.int32, (1, group), 1) + r0).astype(F32)
                onehot = jnp.where(rank_col == slot, 1.0, 0.0).astype(BF16)
                acc_ref[...] += gate_col * _dot(onehot, yacc_ref[pl.ds(r0, group), :].astype(BF16))

            over_groups(expand)

    @pl.when(jnp.logical_and(n_rows > 0, n_rows <= MOE_GROUP_SMALL))
    def _():
        run(MOE_GROUP_SMALL, 1)

    @pl.when(jnp.logical_and(n_rows > MOE_GROUP_SMALL, n_rows <= MOE_GROUP))
    def _():
        run(MOE_GROUP, 1)

    @pl.when(n_rows > MOE_GROUP)
    def _():
        run(MOE_GROUP, (n_rows + (MOE_GROUP - 1)) // MOE_GROUP)

    @pl.when(jnp.logical_and(e == last_e, j == last_j))
    def _():
        o_ref[...] = h_ref[...] + _rms(acc_ref[...], gout_ref[...])


def moe_block(h, g_in, w_router, wg, wu, wd, g_out, tm=MOE_TM, tf=MOE_TF):
    t, d = h.shape
    n_e, _, f = wg.shape
    n_tiles = t // tm
    wr = jnp.pad(w_router, ((0, 0), (0, LANES - n_e)))
    xn, gates, rank, rankt, counts = pl.pallas_call(
        _router_kernel,
        grid=(n_tiles,),
        in_specs=[
            pl.BlockSpec((tm, d), lambda i: (i, 0)),
            pl.BlockSpec((1, d), lambda i: (0, 0)),
            pl.BlockSpec((d, LANES), lambda i: (0, 0)),
        ],
        out_specs=[
            pl.BlockSpec((tm, d), lambda i: (i, 0)),
            pl.BlockSpec((tm, LANES), lambda i: (i, 0)),
            pl.BlockSpec((tm, LANES), lambda i: (i, 0)),
            pl.BlockSpec((1, LANES, tm), lambda i: (i, 0, 0)),
            pl.BlockSpec((1, 1, LANES), lambda i: (i, 0, 0)),
        ],
        out_shape=[
            jax.ShapeDtypeStruct((t, d), BF16),
            jax.ShapeDtypeStruct((t, LANES), F32),
            jax.ShapeDtypeStruct((t, LANES), F32),
            jax.ShapeDtypeStruct((n_tiles, LANES, tm), F32),
            jax.ShapeDtypeStruct((n_tiles, 1, LANES), F32),
        ],
        compiler_params=pltpu.CompilerParams(
            dimension_semantics=("parallel",), vmem_limit_bytes=VMEM_LIMIT),
        name="moe_router",
    )(h, g_in.reshape(1, d), wr)
    n_rows = counts[:, 0, :n_e].astype(jnp.int32)
    max_rows = -(-tm // MOE_GROUP) * MOE_GROUP
    grid_spec = pltpu.PrefetchScalarGridSpec(
        num_scalar_prefetch=1,
        grid=(n_tiles, n_e, f // tf),
        in_specs=[
            pl.BlockSpec((tm, d), lambda i, e, j, ng: (i, 0)),
            pl.BlockSpec((tm, LANES), lambda i, e, j, ng: (i, 0)),
            pl.BlockSpec((tm, LANES), lambda i, e, j, ng: (i, 0)),
            pl.BlockSpec((1, LANES, tm), lambda i, e, j, ng: (i, 0, 0)),
            pl.BlockSpec((1, d, tf), lambda i, e, j, ng: (e, 0, j)),
            pl.BlockSpec((1, d, tf), lambda i, e, j, ng: (e, 0, j)),
            pl.BlockSpec((1, tf, d), lambda i, e, j, ng: (e, j, 0)),
            pl.BlockSpec((tm, d), lambda i, e, j, ng: (i, 0)),
            pl.BlockSpec((1, d), lambda i, e, j, ng: (0, 0)),
        ],
        out_specs=pl.BlockSpec((tm, d), lambda i, e, j, ng: (i, 0)),
        scratch_shapes=[
            pltpu.VMEM((max_rows, d), BF16),
            pltpu.VMEM((max_rows, d), F32),
            pltpu.VMEM((tm, d), F32),
        ],
    )
    return pl.pallas_call(
        _moe_kernel,
        grid_spec=grid_spec,
        out_shape=jax.ShapeDtypeStruct((t, d), F32),
        compiler_params=pltpu.CompilerParams(
            dimension_semantics=("parallel", "arbitrary", "arbitrary"), vmem_limit_bytes=VMEM_LIMIT),
        name="moe_swiglu",
    )(n_rows.reshape(-1), xn, gates, rank, rankt, wg, wu, wd, h, g_out.reshape(1, d))


def kernel(x, norm_g, na_w_qkv, na_rpb, na_w_o, gqa_w_qkv, gqa_q_norm, gqa_k_norm, gqa_w_o,
           ffn_w_gate, ffn_w_up, ffn_w_down, moe_w_router, moe_w_gate, moe_w_up, moe_w_down):
    assert x.shape == (BATCH, SEQ, D_MODEL)
    h = x.reshape(TOKENS, D_MODEL)

    log2e = float(np.log2(np.e))
    col_scale = jnp.concatenate([jnp.full((D_MODEL,), (HEAD_DIM ** -0.5) * log2e, F32),
                                 jnp.ones((2 * D_MODEL,), F32)])
    qkv = norm_matmul(h, norm_g[0, 0], na_w_qkv[0].astype(BF16), col_scale)
    a = neighborhood_attention(qkv, _na_bias_tables(na_rpb[0] * log2e))
    h = proj_norm_res(a, na_w_o[0].astype(BF16), norm_g[0, 1], h)
    h = ffn_block(h, norm_g[0, 2], ffn_w_gate[0].astype(BF16), ffn_w_up[0].astype(BF16),
                  ffn_w_down[0].astype(BF16), norm_g[0, 3])

    nq = GQA_Q_HEADS * HEAD_DIM
    nkv = GQA_KV_HEADS * HEAD_DIM
    wq, wk, wv = gqa_w_qkv[0][:, :nq], gqa_w_qkv[0][:, nq:nq + nkv], gqa_w_qkv[0][:, nq + nkv:]
    qt, k, vt = gqa_qkv(h, norm_g[1, 0], wq, wk, wv, gqa_q_norm[0], gqa_k_norm[0])
    o = flash_attention(qt, k, vt)
    h = proj_norm_res(o, gqa_w_o[0].astype(BF16), norm_g[1, 1], h)
    h = moe_block(h, norm_g[1, 2], moe_w_router[0], moe_w_gate[0].astype(BF16), moe_w_up[0].astype(BF16),
                  moe_w_down[0].astype(BF16), norm_g[1, 3])
    return h.reshape(BATCH, SEQ, D_MODEL)
```

```python
import functools

import jax
import jax.numpy as jnp
import numpy as np
from jax import lax
from jax.experimental import pallas as pl
from jax.experimental.pallas import tpu as pltpu

F32 = jnp.float32
BF16 = jnp.bfloat16

D_MODEL = 1024
BATCH = 2
SEQ = 8192
TOKENS = BATCH * SEQ
GRID_W = 64
GRID_H = SEQ // GRID_W
NA_HEADS = 16
HEAD_DIM = 64
WIN_H = 8
WIN_W = 16
GQA_Q_HEADS = 16
GQA_KV_HEADS = 4
GQA_GROUPS = GQA_Q_HEADS // GQA_KV_HEADS
ROPE_THETA = 10000.0
D_FF = 3584
N_EXPERTS = 8
EPS = 1e-6
LANES = 128
NEG_BIG = -1e30

VMEM_LIMIT = 56 * 1024 * 1024

NA_QROWS = 4
NA_QTOK = NA_QROWS * GRID_W
NA_BAND_BLOCKS = 3
NA_KTOK = NA_BAND_BLOCKS * NA_QTOK
NA_ROW_BLOCKS = GRID_H // NA_QROWS
NA_HEADS_PER_STEP = 4

FA_TQ = 512
FA_TK = 512
MOE_TM = 1024
MOE_TF = 896
MOE_GROUP = 384
MOE_GROUP_SMALL = 256

FA_UNROLL = 4
FA_INIT_KEYS = 16
FA_MAX_JUMP = 24.0
FA_VROWS = 80


def _rms(x, g):
    ms = jnp.mean(x * x, axis=-1, keepdims=True)
    return x * lax.rsqrt(ms + EPS) * g


def _dot(a, b):
    return jnp.dot(a, b, preferred_element_type=F32)


def _dot_nt(a, b):
    return lax.dot_general(a, b, (((1,), (1,)), ((), ())), preferred_element_type=F32)


def _na_qkv_kernel(x_ref, g_ref, w_ref, qt_ref, k_ref, vt_ref, *, q_scale, n_chunk):
    xn = _rms(x_ref[...], g_ref[...]).astype(BF16)
    tm = xn.shape[0]
    d = k_ref.shape[1]

    def store_transposed(dst_ref, y, col0):
        for t0 in range(0, tm, NA_QTOK):
            for c0 in range(0, n_chunk, LANES):
                dst_ref[t0 // NA_QTOK, col0 + c0:col0 + c0 + LANES, :] = (
                    y[t0:t0 + NA_QTOK, c0:c0 + LANES].T.astype(BF16))

    for n0 in range(0, d, n_chunk):
        store_transposed(qt_ref, _dot(xn, w_ref[:, n0:n0 + n_chunk]) * q_scale, n0)
        k_ref[:, n0:n0 + n_chunk] = _dot(xn, w_ref[:, d + n0:d + n0 + n_chunk]).astype(BF16)
        store_transposed(vt_ref, _dot(xn, w_ref[:, 2 * d + n0:2 * d + n0 + n_chunk]), n0)


def na_qkv(x, g, w, q_scale, tm=512, n_chunk=512):
    t, d = x.shape
    n = w.shape[1]
    tiles = tm // NA_QTOK
    transposed = jax.ShapeDtypeStruct((t // NA_QTOK, d, NA_QTOK), BF16)
    return pl.pallas_call(
        functools.partial(_na_qkv_kernel, q_scale=q_scale, n_chunk=n_chunk),
        grid=(t // tm,),
        in_specs=[
            pl.BlockSpec((tm, d), lambda i: (i, 0)),
            pl.BlockSpec((1, d), lambda i: (0, 0)),
            pl.BlockSpec((d, n), lambda i: (0, 0)),
        ],
        out_specs=[
            pl.BlockSpec((tiles, d, NA_QTOK), lambda i: (i, 0, 0)),
            pl.BlockSpec((tm, d), lambda i: (i, 0)),
            pl.BlockSpec((tiles, d, NA_QTOK), lambda i: (i, 0, 0)),
        ],
        out_shape=[transposed, jax.ShapeDtypeStruct((t, d), BF16), transposed],
        compiler_params=pltpu.CompilerParams(
            dimension_semantics=("parallel",), vmem_limit_bytes=VMEM_LIMIT),
        name="na_qkv",
    )(x, g.reshape(1, d), w)


def _proj_norm_res_kernel(a_ref, w_ref, g_ref, h_ref, o_ref):
    y = _dot(a_ref[...], w_ref[...])
    o_ref[...] = h_ref[...] + _rms(y, g_ref[...])


def proj_norm_res(a, w, g, h, tm=512):
    t, k = a.shape
    d = w.shape[1]
    return pl.pallas_call(
        _proj_norm_res_kernel,
        grid=(t // tm,),
        in_specs=[
            pl.BlockSpec((tm, k), lambda i: (i, 0)),
            pl.BlockSpec((k, d), lambda i: (0, 0)),
            pl.BlockSpec((1, d), lambda i: (0, 0)),
            pl.BlockSpec((tm, d), lambda i: (i, 0)),
        ],
        out_specs=pl.BlockSpec((tm, d), lambda i: (i, 0)),
        out_shape=jax.ShapeDtypeStruct((t, d), F32),
        compiler_params=pltpu.CompilerParams(
            dimension_semantics=("parallel",), vmem_limit_bytes=VMEM_LIMIT),
        name="proj_norm_res",
    )(a, w, g.reshape(1, d), h)


def _na_bias_expand_kernel(plane_ref, cols_ref, o_ref):
    case = pl.program_id(0)
    band_rows = NA_BAND_BLOCKS * NA_QROWS
    for i in range(NA_QROWS):
        for m in range(band_rows):
            u = plane_ref[(case * NA_QROWS + i) * band_rows + m]
            o_ref[0, 0, m * GRID_W:(m + 1) * GRID_W, i * GRID_W:(i + 1) * GRID_W] = cols_ref[0, u]


def _na_bias_tables(rpb):
    i = np.arange(NA_QROWS)[:, None]
    m = np.arange(NA_BAND_BLOCKS * NA_QROWS)[None, :]
    ridx, rvalid = [], []
    for rb in (0, 1, NA_ROW_BLOCKS - 1):
        band0 = NA_QROWS * min(max(rb - 1, 0), NA_ROW_BLOCKS - NA_BAND_BLOCKS)
        r = NA_QROWS * rb + i
        kr = band0 + m
        rs = np.clip(r - WIN_H // 2, 0, GRID_H - WIN_H)
        rvalid.append((kr >= rs) & (kr < rs + WIN_H))
        ridx.append(np.clip(kr - r + WIN_H - 1, 0, 2 * WIN_H - 2))
    n_dr = 2 * WIN_H - 1
    plane = np.where(np.stack(rvalid), np.stack(ridx), n_dr)
    c = np.arange(GRID_W)[:, None]
    kc = np.arange(GRID_W)[None, :]
    cs = np.clip(c - WIN_W // 2, 0, GRID_W - WIN_W)
    cvalid = (kc >= cs) & (kc < cs + WIN_W)
    left = GRID_W - WIN_W
    padded = jnp.pad(rpb, ((0, 0), (0, 0), (left, left)))
    cols = jnp.stack([padded[:, :, GRID_W - 1 - cq:2 * GRID_W - 1 - cq] for cq in range(GRID_W)], axis=2)
    cols = jnp.where(cvalid[None, None], cols, NEG_BIG)
    cols = jnp.concatenate([cols, jnp.full_like(cols[:, :1], NEG_BIG)], axis=1)
    cols_t = cols.transpose(0, 1, 3, 2)
    n_cases = plane.shape[0]
    grid_spec = pltpu.PrefetchScalarGridSpec(
        num_scalar_prefetch=1,
        grid=(n_cases, NA_HEADS),
        in_specs=[pl.BlockSpec((1, n_dr + 1, GRID_W, GRID_W), lambda z, h, pr: (h, 0, 0, 0))],
        out_specs=pl.BlockSpec((1, 1, NA_KTOK, NA_QTOK), lambda z, h, pr: (z, h // 2, 0, h % 2)),
    )
    return pl.pallas_call(
        _na_bias_expand_kernel,
        grid_spec=grid_spec,
        out_shape=jax.ShapeDtypeStruct((n_cases, NA_HEADS // 2, NA_KTOK, 2 * NA_QTOK), F32),
        compiler_params=pltpu.CompilerParams(dimension_semantics=("parallel", "parallel")),
        name="na_bias_expand",
    )(jnp.asarray(plane.reshape(-1), jnp.int32), cols_t)


def _na_kernel(qt_ref, k0_ref, k1_ref, k2_ref, vt0_ref, vt1_ref, vt2_ref, bias_ref, o_ref):
    row = lax.broadcasted_iota(jnp.int32, (LANES, 1), 0)
    low = row < HEAD_DIM

    n_pairs = NA_HEADS_PER_STEP // 2

    def scores(pair):
        rows = slice(pair * LANES, (pair + 1) * LANES)
        qt = qt_ref[0, rows, :]
        k = jnp.concatenate([k0_ref[:, rows], k1_ref[:, rows], k2_ref[:, rows]], axis=0)
        zero = jnp.zeros_like(qt)
        qt2 = jnp.concatenate([jnp.where(low, qt, zero), jnp.where(low, zero, qt)], axis=1)
        return _dot(k, qt2)

    s = scores(0)
    for pair in range(n_pairs):
        s_next = scores(pair + 1) if pair + 1 < n_pairs else None
        rows = slice(pair * LANES, (pair + 1) * LANES)
        vt = jnp.concatenate([vt0_ref[0, rows, :], vt1_ref[0, rows, :], vt2_ref[0, rows, :]], axis=1)
        one = jnp.ones_like(vt)
        s = s + bias_ref[0, pair]
        p = jnp.exp2(s - jnp.max(s, axis=0, keepdims=True)).astype(BF16)
        pv_a = _dot(jnp.where(low, vt, one), p[:, :NA_QTOK])
        pv_b = _dot(jnp.where(low, one, vt), p[:, NA_QTOK:])
        both = jnp.concatenate([pv_a[:HEAD_DIM] / pv_a[HEAD_DIM:], pv_b[HEAD_DIM:] / pv_b[:HEAD_DIM]], axis=0)
        o_ref[:, rows] = both.T.astype(o_ref.dtype)
        s = s_next


def neighborhood_attention(qt, k, vt, bias_tab):
    hw = NA_HEADS_PER_STEP * HEAD_DIM
    nh = D_MODEL // hw
    blocks_per_seq = SEQ // NA_QTOK

    def band(rb):
        return jnp.clip(rb - 1, 0, NA_ROW_BLOCKS - NA_BAND_BLOCKS)

    def case(rb):
        return jnp.where(rb == 0, 0, jnp.where(rb == NA_ROW_BLOCKS - 1, 2, 1))

    qt_spec = pl.BlockSpec((1, hw, NA_QTOK), lambda h, b, rb: (b * blocks_per_seq + rb, h, 0))
    k_specs = [pl.BlockSpec((NA_QTOK, hw), lambda h, b, rb, j=j: (b * blocks_per_seq + band(rb) + j, h))
               for j in range(NA_BAND_BLOCKS)]
    vt_specs = [pl.BlockSpec((1, hw, NA_QTOK), lambda h, b, rb, j=j: (b * blocks_per_seq + band(rb) + j, h, 0))
                for j in range(NA_BAND_BLOCKS)]
    bias_spec = pl.BlockSpec((1, NA_HEADS_PER_STEP // 2, NA_KTOK, 2 * NA_QTOK),
                             lambda h, b, rb: (case(rb), h, 0, 0))
    return pl.pallas_call(
        _na_kernel,
        grid=(nh, BATCH, NA_ROW_BLOCKS),
        in_specs=[qt_spec] + k_specs + vt_specs + [bias_spec],
        out_specs=pl.BlockSpec((NA_QTOK, hw), lambda h, b, rb: (b * blocks_per_seq + rb, h)),
        out_shape=jax.ShapeDtypeStruct((TOKENS, D_MODEL), BF16),
        compiler_params=pltpu.CompilerParams(
            dimension_semantics=("parallel", "parallel", "parallel"), vmem_limit_bytes=VMEM_LIMIT),
        name="na_attention",
    )(qt, k, k, k, vt, vt, vt, bias_tab)


def _ffn_kernel(h_ref, gin_ref, wg_ref, wu_ref, wd_ref, gout_ref, o_ref, xn_ref, acc_ref):
    j = pl.program_id(1)

    @pl.when(j == 0)
    def _():
        xn_ref[...] = _rms(h_ref[...], gin_ref[...]).astype(BF16)
        acc_ref[...] = jnp.zeros_like(acc_ref)

    xn = xn_ref[...]
    gate = _dot(xn, wg_ref[...])
    up = _dot(xn, wu_ref[...])
    act = (gate * jax.nn.sigmoid(gate) * up).astype(BF16)
    acc_ref[...] += _dot(act, wd_ref[...])

    @pl.when(j == pl.num_programs(1) - 1)
    def _():
        o_ref[...] = h_ref[...] + _rms(acc_ref[...], gout_ref[...])


def ffn_block(h, g_in, wg, wu, wd, g_out, tm=1024, tf=512):
    t, d = h.shape
    f = wg.shape[1]
    return pl.pallas_call(
        _ffn_kernel,
        grid=(t // tm, f // tf),
        in_specs=[
            pl.BlockSpec((tm, d), lambda i, j: (i, 0)),
            pl.BlockSpec((1, d), lambda i, j: (0, 0)),
            pl.BlockSpec((d, tf), lambda i, j: (0, j)),
            pl.BlockSpec((d, tf), lambda i, j: (0, j)),
            pl.BlockSpec((tf, d), lambda i, j: (j, 0)),
            pl.BlockSpec((1, d), lambda i, j: (0, 0)),
        ],
        out_specs=pl.BlockSpec((tm, d), lambda i, j: (i, 0)),
        out_shape=jax.ShapeDtypeStruct((t, d), F32),
        scratch_shapes=[pltpu.VMEM((tm, d), BF16), pltpu.VMEM((tm, d), F32)],
        compiler_params=pltpu.CompilerParams(
            dimension_semantics=("parallel", "arbitrary"), vmem_limit_bytes=VMEM_LIMIT),
        name="ffn_swiglu",
    )(h, g_in.reshape(1, d), wg, wu, wd, g_out.reshape(1, d))


def _pad_heads(w, n_heads, width):
    d = w.shape[0]
    w = w.reshape(d, n_heads, HEAD_DIM)
    return jnp.pad(w, ((0, 0), (0, 0), (0, width - HEAD_DIM))).reshape(d, n_heads * width)


def _rope_tables():
    quarter = HEAD_DIM // 4
    freqs = ROPE_THETA ** (-jnp.arange(quarter, dtype=F32) / quarter)
    t = jnp.arange(SEQ)
    row_ang = (t // GRID_W).astype(F32)[:, None] * freqs[None, :]
    col_ang = (t % GRID_W).astype(F32)[:, None] * freqs[None, :]
    zeros = jnp.zeros((SEQ, LANES - HEAD_DIM), F32)
    cos = jnp.concatenate([jnp.cos(row_ang), jnp.cos(row_ang), jnp.cos(col_ang), jnp.cos(col_ang), zeros], axis=1)
    sin = jnp.concatenate([-jnp.sin(row_ang), jnp.sin(row_ang), -jnp.sin(col_ang), jnp.sin(col_ang), zeros], axis=1)
    return cos, sin


def _gqa_qkv_kernel(x_ref, g_ref, wq_ref, wk_ref, wv_ref, qg_ref, kg_ref, cos_ref, sin_ref, cost_ref, sint_ref,
                    qt_ref, k_ref, vt_ref):
    xn = _rms(x_ref[...], g_ref[...]).astype(BF16)
    tm = xn.shape[0]
    quarter = HEAD_DIM // 4

    cost = cost_ref[...]
    sint = sint_ref[...]
    qg = qg_ref[...]
    zero_rows = jnp.zeros((LANES - HEAD_DIM, tm), BF16)
    wide = 2 * LANES
    for pair in range(GQA_Q_HEADS // 2):
        if pair % 2 == 0:
            y_wide = _dot(xn, wq_ref[:, pair * LANES:pair * LANES + wide])
        yt = y_wide[:, (pair % 2) * LANES:(pair % 2 + 1) * LANES].T
        for half in range(2):
            y = yt[half * HEAD_DIM:(half + 1) * HEAD_DIM]
            ms = jnp.sum(y * y, axis=0, keepdims=True) * (1.0 / HEAD_DIM)
            y = y * lax.rsqrt(ms + EPS) * qg
            partner = jnp.concatenate(
                [y[quarter:2 * quarter], y[:quarter], y[3 * quarter:], y[2 * quarter:3 * quarter]], axis=0)
            base = (2 * pair + half) * LANES
            qt_ref[0, base:base + HEAD_DIM, :] = (y * cost + partner * sint).astype(BF16)
            qt_ref[0, base + HEAD_DIM:base + LANES, :] = zero_rows

    lane = lax.broadcasted_iota(jnp.int32, (1, LANES), 1)
    first = (lane % (2 * quarter)) < quarter
    cos = cos_ref[...]
    sin = sin_ref[...]
    for hh in range(GQA_KV_HEADS):
        if hh % 2 == 0:
            y_wide = _dot(xn, wk_ref[:, hh * LANES:hh * LANES + wide])
        y = y_wide[:, (hh % 2) * LANES:(hh % 2 + 1) * LANES]
        ms = jnp.sum(y * y, axis=-1, keepdims=True) * (1.0 / HEAD_DIM)
        y = y * lax.rsqrt(ms + EPS) * kg_ref[...]
        partner = jnp.where(first, pltpu.roll(y, LANES - quarter, 1), pltpu.roll(y, quarter, 1))
        k_ref[:, hh * LANES:(hh + 1) * LANES] = (y * cos + partner * sin).astype(BF16)

    fill = FA_VROWS - HEAD_DIM
    ones_then_zeros = jnp.where(lax.broadcasted_iota(jnp.int32, (fill, tm), 0) == 0, 1.0, 0.0).astype(BF16)
    assert GQA_KV_HEADS * HEAD_DIM == wide
    v_wide = _dot(xn, wv_ref[...])
    for pair in range(GQA_KV_HEADS // 2):
        vt = v_wide[:, pair * LANES:(pair + 1) * LANES].T
        for half in range(2):
            base = (2 * pair + half) * FA_VROWS
            vt_ref[0, base:base + HEAD_DIM, :] = vt[half * HEAD_DIM:(half + 1) * HEAD_DIM].astype(BF16)
            vt_ref[0, base + HEAD_DIM:base + FA_VROWS, :] = ones_then_zeros


def gqa_qkv(h, g, wq, wk, wv, q_gain, k_gain):
    t, d = h.shape
    tm = FA_TK
    seq_blocks = SEQ // tm
    wqb = wq.astype(BF16)
    wkp = _pad_heads(wk, GQA_KV_HEADS, LANES).astype(BF16)
    wvb = wv.astype(BF16)
    cos, sin = _rope_tables()
    q_scale = (HEAD_DIM ** -0.5) * float(np.log2(np.e))
    cost, sint = (cos[:, :HEAD_DIM] * q_scale).T, (sin[:, :HEAD_DIM] * q_scale).T
    qg = q_gain.reshape(HEAD_DIM, 1)
    kg = jnp.pad(k_gain, (0, LANES - HEAD_DIM)).reshape(1, LANES)
    const = lambda shape: pl.BlockSpec(shape, lambda i: (0,) * len(shape))
    return pl.pallas_call(
        _gqa_qkv_kernel,
        grid=(t // tm,),
        in_specs=[
            pl.BlockSpec((tm, d), lambda i: (i, 0)),
            const((1, d)),
            const(wqb.shape),
            const(wkp.shape),
            const(wvb.shape),
            const((HEAD_DIM, 1)),
            const((1, LANES)),
            pl.BlockSpec((tm, LANES), lambda i: (i % seq_blocks, 0)),
            pl.BlockSpec((tm, LANES), lambda i: (i % seq_blocks, 0)),
            pl.BlockSpec((HEAD_DIM, tm), lambda i: (0, i % seq_blocks)),
            pl.BlockSpec((HEAD_DIM, tm), lambda i: (0, i % seq_blocks)),
        ],
        out_specs=[
            pl.BlockSpec((1, GQA_Q_HEADS * LANES, tm), lambda i: (i, 0, 0)),
            pl.BlockSpec((tm, GQA_KV_HEADS * LANES), lambda i: (i, 0)),
            pl.BlockSpec((1, GQA_KV_HEADS * FA_VROWS, tm), lambda i: (i, 0, 0)),
        ],
        out_shape=[
            jax.ShapeDtypeStruct((t // tm, GQA_Q_HEADS * LANES, tm), BF16),
            jax.ShapeDtypeStruct((t, GQA_KV_HEADS * LANES), BF16),
            jax.ShapeDtypeStruct((t // tm, GQA_KV_HEADS * FA_VROWS, tm), BF16),
        ],
        compiler_params=pltpu.CompilerParams(
            dimension_semantics=("parallel",), vmem_limit_bytes=VMEM_LIMIT),
        name="gqa_qkv",
    )(h, g.reshape(1, d), wqb, wkp, wvb, qg, kg, cos, sin, cost, sint)


def _flash_kernel(qt_ref, k_ref, vt_ref, o_ref, m_ref, acc_ref, jump_ref):
    n_kv = SEQ // FA_TK

    def scores(j, g):
        kj = k_ref[pl.ds(pl.multiple_of(j * FA_TK, FA_TK), FA_TK), :]
        return _dot(kj, qt_ref[0, g * LANES:(g + 1) * LANES, :])

    def reset():
        m_ref[...] = jnp.full(m_ref.shape, -jnp.inf, F32)
        acc_ref[...] = jnp.zeros_like(acc_ref)

    def step_exact(j, g, s):
        m_prev = m_ref[g]
        m_new = jnp.maximum(m_prev, jnp.max(s, axis=0, keepdims=True))
        p = jnp.exp2(s - m_new)
        acc_ref[g] = jnp.exp2(m_prev - m_new) * acc_ref[g] + _dot(vt_ref[j], p.astype(BF16))
        m_ref[g] = m_new

    def step_lagged(j, g, s):
        m_prev = m_ref[g]
        p = jnp.exp2(s - m_prev)
        block_max = jnp.max(s, axis=0, keepdims=True)
        m_new = jnp.maximum(m_prev, block_max)
        acc_ref[g] = jnp.exp2(m_prev - m_new) * (acc_ref[g] + _dot(vt_ref[j], p.astype(BF16)))
        m_ref[g] = m_new
        jump_ref[g] = jnp.maximum(jump_ref[g], block_max - m_prev)

    def sweep(j, s, step):
        for g in range(GQA_GROUPS):
            if g + 1 < GQA_GROUPS:
                s_next = scores(j, g + 1)
            else:
                s_next = scores(jnp.minimum(j + 1, n_kv - 1), 0)
            step(j, g, s)
            s = s_next
        return s

    acc_ref[...] = jnp.zeros_like(acc_ref)
    jump_ref[...] = jnp.zeros_like(jump_ref)
    k_first = k_ref[0:FA_INIT_KEYS, :]
    for g in range(GQA_GROUPS):
        m_ref[g] = jnp.max(_dot(k_first, qt_ref[0, g * LANES:(g + 1) * LANES, :]), axis=0, keepdims=True)
    lax.fori_loop(0, n_kv, lambda j, s: sweep(j, s, step_lagged), scores(0, 0), unroll=FA_UNROLL)

    @pl.when(jnp.max(jump_ref[...]) > FA_MAX_JUMP)
    def _():
        reset()

        def body(j, carry):
            for g in range(GQA_GROUPS):
                step_exact(j, g, scores(j, g))
            return carry

        lax.fori_loop(0, n_kv, body, 0)

    for pair in range(GQA_GROUPS // 2):
        halves = []
        for g in (2 * pair, 2 * pair + 1):
            acc = acc_ref[g]
            halves.append(acc[:HEAD_DIM] / acc[HEAD_DIM:HEAD_DIM + 1])
        o_ref[:, pair * LANES:(pair + 1) * LANES] = jnp.concatenate(halves, axis=0).T.astype(o_ref.dtype)


def flash_attention(qt, k, vt):
    q_blocks = SEQ // FA_TQ
    q_per_tile = FA_TK // FA_TQ
    kv_blocks = SEQ // FA_TK
    gw = GQA_GROUPS * LANES
    return pl.pallas_call(
        _flash_kernel,
        grid=(BATCH, GQA_KV_HEADS, q_blocks),
        in_specs=[
            pl.BlockSpec((1, gw, FA_TQ), lambda b, kh, i: (b * kv_blocks + i // q_per_tile, kh, i % q_per_tile)),
            pl.BlockSpec((SEQ, LANES), lambda b, kh, i: (b, kh)),
            pl.BlockSpec((kv_blocks, FA_VROWS, FA_TK), lambda b, kh, i: (b, kh, 0)),
        ],
        out_specs=pl.BlockSpec((FA_TQ, GQA_GROUPS * HEAD_DIM), lambda b, kh, i: (b * q_blocks + i, kh)),
        out_shape=jax.ShapeDtypeStruct((TOKENS, D_MODEL), BF16),
        scratch_shapes=[
            pltpu.VMEM((GQA_GROUPS, 1, FA_TQ), F32),
            pltpu.VMEM((GQA_GROUPS, FA_VROWS, FA_TQ), F32),
            pltpu.VMEM((GQA_GROUPS, 1, FA_TQ), F32),
        ],
        compiler_params=pltpu.CompilerParams(
            dimension_semantics=("parallel", "parallel", "parallel"), vmem_limit_bytes=VMEM_LIMIT),
        name="gqa_flash",
    )(qt, k, vt)


def _router_kernel(h_ref, gin_ref, wr_ref, xn_ref, gates_ref, rank_ref, rankt_ref, count_ref):
    tm = h_ref.shape[0]
    lane = lax.broadcasted_iota(jnp.int32, (1, LANES), 1)
    x = _rms(h_ref[...], gin_ref[...])
    xn_ref[...] = x.astype(BF16)
    logits = jnp.dot(x, wr_ref[...], preferred_element_type=F32, precision=lax.Precision.HIGHEST)
    logits = jnp.where(lane < N_EXPERTS, logits, -jnp.inf)
    m1 = jnp.max(logits, axis=-1, keepdims=True)
    i1 = jnp.min(jnp.where(logits == m1, lane, LANES), axis=-1, keepdims=True)
    rest = jnp.where(lane == i1, -jnp.inf, logits)
    m2 = jnp.max(rest, axis=-1, keepdims=True)
    i2 = jnp.min(jnp.where(rest == m2, lane, LANES), axis=-1, keepdims=True)
    e2 = jnp.exp(m2 - m1)
    denom = 1.0 + e2
    gates_ref[...] = jnp.where(lane == i1, 1.0 / denom, 0.0) + jnp.where(lane == i2, e2 / denom, 0.0)
    chosen = jnp.logical_or(lane == i1, lane == i2)
    earlier = lax.broadcasted_iota(jnp.int32, (tm, tm), 1) < lax.broadcasted_iota(jnp.int32, (tm, tm), 0)
    prefix = _dot(jnp.where(earlier, 1.0, 0.0).astype(BF16), jnp.where(chosen, 1.0, 0.0).astype(BF16))
    rank = jnp.where(chosen, prefix, -1.0)
    rank_ref[...] = rank
    rankt_ref[0] = rank.T
    count_ref[0] = jnp.sum(jnp.where(chosen, 1.0, 0.0), axis=0, keepdims=True)


def _moe_kernel(cnt_ref, xn_ref, gates_ref, rank_ref, rankt_ref, wg_ref, wu_ref, wd_ref, h_ref, gout_ref, o_ref,
                xs_ref, yacc_ref, acc_ref):
    i = pl.program_id(0)
    e = pl.program_id(1)
    j = pl.program_id(2)
    last_e = pl.num_programs(1) - 1
    last_j = pl.num_programs(2) - 1
    n_rows = cnt_ref[i * N_EXPERTS + e]

    @pl.when(jnp.logical_and(e == 0, j == 0))
    def _():
        acc_ref[...] = jnp.zeros_like(acc_ref)

    def run(group, n_groups):
        def over_groups(fn):
            if isinstance(n_groups, int):
                for c in range(n_groups):
                    fn(c * group)
            else:
                def body(c, carry):
                    fn(pl.multiple_of(c * group, LANES))
                    return carry

                lax.fori_loop(0, n_groups, body, 0)

        @pl.when(j == 0)
        def _():
            rank_row = rankt_ref[0, pl.ds(e, 1), :]

            def compact(r0):
                slot = (lax.broadcasted_iota(jnp.int32, (group, 1), 0) + r0).astype(F32)
                onehot = jnp.where(rank_row == slot, 1.0, 0.0).astype(BF16)
                xs_ref[pl.ds(r0, group), :] = _dot(onehot, xn_ref[...]).astype(BF16)
                yacc_ref[pl.ds(r0, group), :] = jnp.zeros((group, yacc_ref.shape[1]), F32)

            over_groups(compact)

        def expert(r0):
            xs = xs_ref[pl.ds(r0, group), :]
            gate = _dot(xs, wg_ref[0])
            up = _dot(xs, wu_ref[0])
            act = (gate * jax.nn.sigmoid(gate) * up).astype(BF16)
            yacc_ref[pl.ds(r0, group), :] += _dot(act, wd_ref[0])

        over_groups(expert)

        @pl.when(j == last_j)
        def _():
            lane = lax.broadcasted_iota(jnp.int32, (1, LANES), 1)
            rank_col = jnp.sum(jnp.where(lane == e, rank_ref[...], 0.0), axis=-1, keepdims=True)
            gate_col = jnp.sum(jnp.where(lane == e, gates_ref[...], 0.0), axis=-1, keepdims=True)

            def expand(r0):
                slot = (lax.broadcasted_iota(jnp.int32, (1, group), 1) + r0).astype(F32)
                onehot = jnp.where(rank_col == slot, 1.0, 0.0).astype(BF16)
                acc_ref[...] += gate_col * _dot(onehot, yacc_ref[pl.ds(r0, group), :].astype(BF16))

            over_groups(expand)

    @pl.when(jnp.logical_and(n_rows > 0, n_rows <= MOE_GROUP_SMALL))
    def _():
        run(MOE_GROUP_SMALL, 1)

    @pl.when(jnp.logical_and(n_rows > MOE_GROUP_SMALL, n_rows <= MOE_GROUP))
    def _():
        run(MOE_GROUP, 1)

    @pl.when(n_rows > MOE_GROUP)
    def _():
        run(MOE_GROUP, (n_rows + (MOE_GROUP - 1)) // MOE_GROUP)

    @pl.when(jnp.logical_and(e == last_e, j == last_j))
    def _():
        o_ref[...] = h_ref[...] + _rms(acc_ref[...], gout_ref[...])


def moe_block(h, g_in, w_router, wg, wu, wd, g_out, tm=MOE_TM, tf=MOE_TF):
    t, d = h.shape
    n_e, _, f = wg.shape
    n_tiles = t // tm
    wr = jnp.pad(w_router, ((0, 0), (0, LANES - n_e)))
    xn, gates, rank, rankt, counts = pl.pallas_call(
        _router_kernel,
        grid=(n_tiles,),
        in_specs=[
            pl.BlockSpec((tm, d), lambda i: (i, 0)),
            pl.BlockSpec((1, d), lambda i: (0, 0)),
            pl.BlockSpec((d, LANES), lambda i: (0, 0)),
        ],
        out_specs=[
            pl.BlockSpec((tm, d), lambda i: (i, 0)),
            pl.BlockSpec((tm, LANES), lambda i: (i, 0)),
            pl.BlockSpec((tm, LANES), lambda i: (i, 0)),
            pl.BlockSpec((1, LANES, tm), lambda i: (i, 0, 0)),
            pl.BlockSpec((1, 1, LANES), lambda i: (i, 0, 0)),
        ],
        out_shape=[
            jax.ShapeDtypeStruct((t, d), BF16),
            jax.ShapeDtypeStruct((t, LANES), F32),
            jax.ShapeDtypeStruct((t, LANES), F32),
            jax.ShapeDtypeStruct((n_tiles, LANES, tm), F32),
            jax.ShapeDtypeStruct((n_tiles, 1, LANES), F32),
        ],
        compiler_params=pltpu.CompilerParams(
            dimension_semantics=("parallel",), vmem_limit_bytes=VMEM_LIMIT),
        name="moe_router",
    )(h, g_in.reshape(1, d), wr)
    n_rows = counts[:, 0, :n_e].astype(jnp.int32)
    max_rows = -(-tm // MOE_GROUP) * MOE_GROUP
    grid_spec = pltpu.PrefetchScalarGridSpec(
        num_scalar_prefetch=1,
        grid=(n_tiles, n_e, f // tf),
        in_specs=[
            pl.BlockSpec((tm, d), lambda i, e, j, ng: (i, 0)),
            pl.BlockSpec((tm, LANES), lambda i, e, j, ng: (i, 0)),
            pl.BlockSpec((tm, LANES), lambda i, e, j, ng: (i, 0)),
            pl.BlockSpec((1, LANES, tm), lambda i, e, j, ng: (i, 0, 0)),
            pl.BlockSpec((1, d, tf), lambda i, e, j, ng: (e, 0, j)),
            pl.BlockSpec((1, d, tf), lambda i, e, j, ng: (e, 0, j)),
            pl.BlockSpec((1, tf, d), lambda i, e, j, ng: (e, j, 0)),
            pl.BlockSpec((tm, d), lambda i, e, j, ng: (i, 0)),
            pl.BlockSpec((1, d), lambda i, e, j, ng: (0, 0)),
        ],
        out_specs=pl.BlockSpec((tm, d), lambda i, e, j, ng: (i, 0)),
        scratch_shapes=[
            pltpu.VMEM((max_rows, d), BF16),
            pltpu.VMEM((max_rows, d), F32),
            pltpu.VMEM((tm, d), F32),
        ],
    )
    return pl.pallas_call(
        _moe_kernel,
        grid_spec=grid_spec,
        out_shape=jax.ShapeDtypeStruct((t, d), F32),
        compiler_params=pltpu.CompilerParams(
            dimension_semantics=("parallel", "arbitrary", "arbitrary"), vmem_limit_bytes=VMEM_LIMIT),
        name="moe_swiglu",
    )(n_rows.reshape(-1), xn, gates, rank, rankt, wg, wu, wd, h, g_out.reshape(1, d))


def kernel(x, norm_g, na_w_qkv, na_rpb, na_w_o, gqa_w_qkv, gqa_q_norm, gqa_k_norm, gqa_w_o,
           ffn_w_gate, ffn_w_up, ffn_w_down, moe_w_router, moe_w_gate, moe_w_up, moe_w_down):
    assert x.shape == (BATCH, SEQ, D_MODEL)
    h = x.reshape(TOKENS, D_MODEL)

    log2e = float(np.log2(np.e))
    qt0, k0, vt0 = na_qkv(h, norm_g[0, 0], na_w_qkv[0].astype(BF16), (HEAD_DIM ** -0.5) * log2e)
    a = neighborhood_attention(qt0, k0, vt0, _na_bias_tables(na_rpb[0] * log2e))
    h = proj_norm_res(a, na_w_o[0].astype(BF16), norm_g[0, 1], h)
    h = ffn_block(h, norm_g[0, 2], ffn_w_gate[0].astype(BF16), ffn_w_up[0].astype(BF16),
                  ffn_w_down[0].astype(BF16), norm_g[0, 3])

    nq = GQA_Q_HEADS * HEAD_DIM
    nkv = GQA_KV_HEADS * HEAD_DIM
    wq, wk, wv = gqa_w_qkv[0][:, :nq], gqa_w_qkv[0][:, nq:nq + nkv], gqa_w_qkv[0][:, nq + nkv:]
    qt, k, vt = gqa_qkv(h, norm_g[1, 0], wq, wk, wv, gqa_q_norm[0], gqa_k_norm[0])
    o = flash_attention(qt, k, vt)
    h = proj_norm_res(o, gqa_w_o[0].astype(BF16), norm_g[1, 1], h)
    h = moe_block(h, norm_g[1, 2], moe_w_router[0], moe_w_gate[0].astype(BF16), moe_w_up[0].astype(BF16),
                  moe_w_down[0].astype(BF16), norm_g[1, 3])
    return h.reshape(BATCH, SEQ, D_MODEL)
```

```python
import functools

import jax
import jax.numpy as jnp
import numpy as np
from jax import lax
from jax.experimental import pallas as pl
from jax.experimental.pallas import tpu as pltpu

F32 = jnp.float32
BF16 = jnp.bfloat16

D_MODEL = 1024
BATCH = 2
SEQ = 8192
TOKENS = BATCH * SEQ
GRID_W = 64
GRID_H = SEQ // GRID_W
NA_HEADS = 16
HEAD_DIM = 64
WIN_H = 8
WIN_W = 16
GQA_Q_HEADS = 16
GQA_KV_HEADS = 4
GQA_GROUPS = GQA_Q_HEADS // GQA_KV_HEADS
ROPE_THETA = 10000.0
D_FF = 3584
N_EXPERTS = 8
EPS = 1e-6
LANES = 128
NEG_BIG = -1e30

VMEM_LIMIT = 56 * 1024 * 1024

NA_QROWS = 4
NA_QTOK = NA_QROWS * GRID_W
NA_BAND_BLOCKS = 3
NA_KTOK = NA_BAND_BLOCKS * NA_QTOK
NA_ROW_BLOCKS = GRID_H // NA_QROWS
NA_HEADS_PER_STEP = 4

FA_TQ = 512
FA_TK = 512
MOE_TM = 1024
MOE_TF = 896
MOE_GROUP = 384
MOE_GROUP_SMALL = 256

FA_UNROLL = 4
FA_INIT_KEYS = 16
FA_MAX_JUMP = 24.0
FA_VROWS = 80


def _rms(x, g):
    ms = jnp.mean(x * x, axis=-1, keepdims=True)
    return x * lax.rsqrt(ms + EPS) * g


def _dot(a, b):
    return jnp.dot(a, b, preferred_element_type=F32)


def _dot_nt(a, b):
    return lax.dot_general(a, b, (((1,), (1,)), ((), ())), preferred_element_type=F32)


def _na_qkv_kernel(x_ref, g_ref, w_ref, qt_ref, k_ref, vt_ref, *, q_scale, n_chunk):
    xn = _rms(x_ref[...], g_ref[...]).astype(BF16)
    tm = xn.shape[0]
    d = k_ref.shape[1]

    def store_transposed(dst_ref, y, col0):
        for t0 in range(0, tm, NA_QTOK):
            for c0 in range(0, n_chunk, LANES):
                dst_ref[t0 // NA_QTOK, col0 + c0:col0 + c0 + LANES, :] = (
                    y[t0:t0 + NA_QTOK, c0:c0 + LANES].T.astype(BF16))

    for n0 in range(0, d, n_chunk):
        store_transposed(qt_ref, _dot(xn, w_ref[:, n0:n0 + n_chunk]) * q_scale, n0)
        k_ref[:, n0:n0 + n_chunk] = _dot(xn, w_ref[:, d + n0:d + n0 + n_chunk]).astype(BF16)
        store_transposed(vt_ref, _dot(xn, w_ref[:, 2 * d + n0:2 * d + n0 + n_chunk]), n0)


def na_qkv(x, g, w, q_scale, tm=512, n_chunk=512):
    t, d = x.shape
    n = w.shape[1]
    tiles = tm // NA_QTOK
    transposed = jax.ShapeDtypeStruct((t // NA_QTOK, d, NA_QTOK), BF16)
    return pl.pallas_call(
        functools.partial(_na_qkv_kernel, q_scale=q_scale, n_chunk=n_chunk),
        grid=(t // tm,),
        in_specs=[
            pl.BlockSpec((tm, d), lambda i: (i, 0)),
            pl.BlockSpec((1, d), lambda i: (0, 0)),
            pl.BlockSpec((d, n), lambda i: (0, 0)),
        ],
        out_specs=[
            pl.BlockSpec((tiles, d, NA_QTOK), lambda i: (i, 0, 0)),
            pl.BlockSpec((tm, d), lambda i: (i, 0)),
            pl.BlockSpec((tiles, d, NA_QTOK), lambda i: (i, 0, 0)),
        ],
        out_shape=[transposed, jax.ShapeDtypeStruct((t, d), BF16), transposed],
        compiler_params=pltpu.CompilerParams(
            dimension_semantics=("parallel",), vmem_limit_bytes=VMEM_LIMIT),
        name="na_qkv",
    )(x, g.reshape(1, d), w)


def _proj_norm_res_kernel(a_ref, w_ref, g_ref, h_ref, o_ref):
    y = _dot(a_ref[...], w_ref[...])
    o_ref[...] = h_ref[...] + _rms(y, g_ref[...])


def proj_norm_res(a, w, g, h, tm=512):
    t, k = a.shape
    d = w.shape[1]
    return pl.pallas_call(
        _proj_norm_res_kernel,
        grid=(t // tm,),
        in_specs=[
            pl.BlockSpec((tm, k), lambda i: (i, 0)),
            pl.BlockSpec((k, d), lambda i: (0, 0)),
            pl.BlockSpec((1, d), lambda i: (0, 0)),
            pl.BlockSpec((tm, d), lambda i: (i, 0)),
        ],
        out_specs=pl.BlockSpec((tm, d), lambda i: (i, 0)),
        out_shape=jax.ShapeDtypeStruct((t, d), F32),
        compiler_params=pltpu.CompilerParams(
            dimension_semantics=("parallel",), vmem_limit_bytes=VMEM_LIMIT),
        name="proj_norm_res",
    )(a, w, g.reshape(1, d), h)


def _na_bias_expand_kernel(plane_ref, cols_ref, o_ref):
    case = pl.program_id(0)
    band_rows = NA_BAND_BLOCKS * NA_QROWS
    for i in range(NA_QROWS):
        for m in range(band_rows):
            u = plane_ref[(case * NA_QROWS + i) * band_rows + m]
            o_ref[0, 0, m * GRID_W:(m + 1) * GRID_W, i * GRID_W:(i + 1) * GRID_W] = cols_ref[0, u]


def _na_bias_tables(rpb):
    i = np.arange(NA_QROWS)[:, None]
    m = np.arange(NA_BAND_BLOCKS * NA_QROWS)[None, :]
    ridx, rvalid = [], []
    for rb in (0, 1, NA_ROW_BLOCKS - 1):
        band0 = NA_QROWS * min(max(rb - 1, 0), NA_ROW_BLOCKS - NA_BAND_BLOCKS)
        r = NA_QROWS * rb + i
        kr = band0 + m
        rs = np.clip(r - WIN_H // 2, 0, GRID_H - WIN_H)
        rvalid.append((kr >= rs) & (kr < rs + WIN_H))
        ridx.append(np.clip(kr - r + WIN_H - 1, 0, 2 * WIN_H - 2))
    n_dr = 2 * WIN_H - 1
    plane = np.where(np.stack(rvalid), np.stack(ridx), n_dr)
    c = np.arange(GRID_W)[:, None]
    kc = np.arange(GRID_W)[None, :]
    cs = np.clip(c - WIN_W // 2, 0, GRID_W - WIN_W)
    cvalid = (kc >= cs) & (kc < cs + WIN_W)
    valid_t = cvalid.T
    pick = ((kc.T - c.T + WIN_W - 1)[None] == np.arange(2 * WIN_W - 1)[:, None, None]) & valid_t[None]
    cols_t = jnp.einsum('hud,dkc->hukc', rpb, jnp.asarray(pick, F32), precision=lax.Precision.HIGHEST)
    cols_t = jnp.where(valid_t[None, None], cols_t, NEG_BIG)
    cols_t = jnp.concatenate([cols_t, jnp.full_like(cols_t[:, :1], NEG_BIG)], axis=1)
    n_cases = plane.shape[0]
    grid_spec = pltpu.PrefetchScalarGridSpec(
        num_scalar_prefetch=1,
        grid=(n_cases, NA_HEADS),
        in_specs=[pl.BlockSpec((1, n_dr + 1, GRID_W, GRID_W), lambda z, h, pr: (h, 0, 0, 0))],
        out_specs=pl.BlockSpec((1, 1, NA_KTOK, NA_QTOK), lambda z, h, pr: (z, h // 2, 0, h % 2)),
    )
    return pl.pallas_call(
        _na_bias_expand_kernel,
        grid_spec=grid_spec,
        out_shape=jax.ShapeDtypeStruct((n_cases, NA_HEADS // 2, NA_KTOK, 2 * NA_QTOK), F32),
        compiler_params=pltpu.CompilerParams(dimension_semantics=("parallel", "parallel")),
        name="na_bias_expand",
    )(jnp.asarray(plane.reshape(-1), jnp.int32), cols_t)


def _na_kernel(qt_ref, k0_ref, k1_ref, k2_ref, vt0_ref, vt1_ref, vt2_ref, bias_ref, o_ref):
    row = lax.broadcasted_iota(jnp.int32, (LANES, 1), 0)
    low = row < HEAD_DIM

    n_pairs = NA_HEADS_PER_STEP // 2

    def scores(pair):
        rows = slice(pair * LANES, (pair + 1) * LANES)
        qt = qt_ref[0, rows, :]
        k = jnp.concatenate([k0_ref[:, rows], k1_ref[:, rows], k2_ref[:, rows]], axis=0)
        zero = jnp.zeros_like(qt)
        qt2 = jnp.concatenate([jnp.where(low, qt, zero), jnp.where(low, zero, qt)], axis=1)
        return _dot(k, qt2)

    s = scores(0)
    for pair in range(n_pairs):
        s_next = scores(pair + 1) if pair + 1 < n_pairs else None
        rows = slice(pair * LANES, (pair + 1) * LANES)
        vt = jnp.concatenate([vt0_ref[0, rows, :], vt1_ref[0, rows, :], vt2_ref[0, rows, :]], axis=1)
        one = jnp.ones_like(vt)
        s = s + bias_ref[0, pair]
        p = jnp.exp2(s - jnp.max(s, axis=0, keepdims=True)).astype(BF16)
        pv_a = _dot(jnp.where(low, vt, one), p[:, :NA_QTOK])
        pv_b = _dot(jnp.where(low, one, vt), p[:, NA_QTOK:])
        both = jnp.concatenate([pv_a[:HEAD_DIM] / pv_a[HEAD_DIM:], pv_b[HEAD_DIM:] / pv_b[:HEAD_DIM]], axis=0)
        o_ref[:, rows] = both.T.astype(o_ref.dtype)
        s = s_next


def neighborhood_attention(qt, k, vt, bias_tab):
    hw = NA_HEADS_PER_STEP * HEAD_DIM
    nh = D_MODEL // hw
    blocks_per_seq = SEQ // NA_QTOK

    def band(rb):
        return jnp.clip(rb - 1, 0, NA_ROW_BLOCKS - NA_BAND_BLOCKS)

    def case(rb):
        return jnp.where(rb == 0, 0, jnp.where(rb == NA_ROW_BLOCKS - 1, 2, 1))

    qt_spec = pl.BlockSpec((1, hw, NA_QTOK), lambda h, b, rb: (b * blocks_per_seq + rb, h, 0))
    k_specs = [pl.BlockSpec((NA_QTOK, hw), lambda h, b, rb, j=j: (b * blocks_per_seq + band(rb) + j, h))
               for j in range(NA_BAND_BLOCKS)]
    vt_specs = [pl.BlockSpec((1, hw, NA_QTOK), lambda h, b, rb, j=j: (b * blocks_per_seq + band(rb) + j, h, 0))
                for j in range(NA_BAND_BLOCKS)]
    bias_spec = pl.BlockSpec((1, NA_HEADS_PER_STEP // 2, NA_KTOK, 2 * NA_QTOK),
                             lambda h, b, rb: (case(rb), h, 0, 0))
    return pl.pallas_call(
        _na_kernel,
        grid=(nh, BATCH, NA_ROW_BLOCKS),
        in_specs=[qt_spec] + k_specs + vt_specs + [bias_spec],
        out_specs=pl.BlockSpec((NA_QTOK, hw), lambda h, b, rb: (b * blocks_per_seq + rb, h)),
        out_shape=jax.ShapeDtypeStruct((TOKENS, D_MODEL), BF16),
        compiler_params=pltpu.CompilerParams(
            dimension_semantics=("parallel", "parallel", "parallel"), vmem_limit_bytes=VMEM_LIMIT),
        name="na_attention",
    )(qt, k, k, k, vt, vt, vt, bias_tab)


def _ffn_kernel(h_ref, gin_ref, wg_ref, wu_ref, wd_ref, gout_ref, o_ref, xn_ref, acc_ref):
    j = pl.program_id(1)

    @pl.when(j == 0)
    def _():
        xn_ref[...] = _rms(h_ref[...], gin_ref[...]).astype(BF16)
        acc_ref[...] = jnp.zeros_like(acc_ref)

    xn = xn_ref[...]
    gate = _dot(xn, wg_ref[...])
    up = _dot(xn, wu_ref[...])
    act = (gate * jax.nn.sigmoid(gate) * up).astype(BF16)
    acc_ref[...] += _dot(act, wd_ref[...])

    @pl.when(j == pl.num_programs(1) - 1)
    def _():
        o_ref[...] = h_ref[...] + _rms(acc_ref[...], gout_ref[...])


def ffn_block(h, g_in, wg, wu, wd, g_out, tm=MOE_TM, tf=MOE_TF):
    t, d = h.shape
    f = wg.shape[1]
    return pl.pallas_call(
        _ffn_kernel,
        grid=(t // tm, f // tf),
        in_specs=[
            pl.BlockSpec((tm, d), lambda i, j: (i, 0)),
            pl.BlockSpec((1, d), lambda i, j: (0, 0)),
            pl.BlockSpec((d, tf), lambda i, j: (0, j)),
            pl.BlockSpec((d, tf), lambda i, j: (0, j)),
            pl.BlockSpec((tf, d), lambda i, j: (j, 0)),
            pl.BlockSpec((1, d), lambda i, j: (0, 0)),
        ],
        out_specs=pl.BlockSpec((tm, d), lambda i, j: (i, 0)),
        out_shape=jax.ShapeDtypeStruct((t, d), F32),
        scratch_shapes=[pltpu.VMEM((tm, d), BF16), pltpu.VMEM((tm, d), F32)],
        compiler_params=pltpu.CompilerParams(
            dimension_semantics=("parallel", "arbitrary"), vmem_limit_bytes=VMEM_LIMIT),
        name="ffn_swiglu",
    )(h, g_in.reshape(1, d), wg, wu, wd, g_out.reshape(1, d))


def _pad_heads(w, n_heads, width):
    d = w.shape[0]
    w = w.reshape(d, n_heads, HEAD_DIM)
    return jnp.pad(w, ((0, 0), (0, 0), (0, width - HEAD_DIM))).reshape(d, n_heads * width)


def _rope_tables():
    quarter = HEAD_DIM // 4
    freqs = ROPE_THETA ** (-jnp.arange(quarter, dtype=F32) / quarter)
    t = jnp.arange(SEQ)
    row_ang = (t // GRID_W).astype(F32)[:, None] * freqs[None, :]
    col_ang = (t % GRID_W).astype(F32)[:, None] * freqs[None, :]
    zeros = jnp.zeros((SEQ, LANES - HEAD_DIM), F32)
    cos = jnp.concatenate([jnp.cos(row_ang), jnp.cos(row_ang), jnp.cos(col_ang), jnp.cos(col_ang), zeros], axis=1)
    sin = jnp.concatenate([-jnp.sin(row_ang), jnp.sin(row_ang), -jnp.sin(col_ang), jnp.sin(col_ang), zeros], axis=1)
    return cos, sin


def _gqa_qkv_kernel(x_ref, g_ref, wq_ref, wk_ref, wv_ref, qg_ref, kg_ref, cos_ref, sin_ref, cost_ref, sint_ref,
                    qt_ref, k_ref, vt_ref):
    xn = _rms(x_ref[...], g_ref[...]).astype(BF16)
    tm = xn.shape[0]
    quarter = HEAD_DIM // 4

    cost = cost_ref[...]
    sint = sint_ref[...]
    qg = qg_ref[...]
    zero_rows = jnp.zeros((LANES - HEAD_DIM, tm), BF16)
    wide = 2 * LANES
    for pair in range(GQA_Q_HEADS // 2):
        if pair % 2 == 0:
            y_wide = _dot(xn, wq_ref[:, pair * LANES:pair * LANES + wide])
        yt = y_wide[:, (pair % 2) * LANES:(pair % 2 + 1) * LANES].T
        for half in range(2):
            y = yt[half * HEAD_DIM:(half + 1) * HEAD_DIM]
            ms = jnp.sum(y * y, axis=0, keepdims=True) * (1.0 / HEAD_DIM)
            y = y * lax.rsqrt(ms + EPS) * qg
            partner = jnp.concatenate(
                [y[quarter:2 * quarter], y[:quarter], y[3 * quarter:], y[2 * quarter:3 * quarter]], axis=0)
            base = (2 * pair + half) * LANES
            qt_ref[0, base:base + HEAD_DIM, :] = (y * cost + partner * sint).astype(BF16)
            qt_ref[0, base + HEAD_DIM:base + LANES, :] = zero_rows

    lane = lax.broadcasted_iota(jnp.int32, (1, LANES), 1)
    first = (lane % (2 * quarter)) < quarter
    cos = cos_ref[...]
    sin = sin_ref[...]
    for hh in range(GQA_KV_HEADS):
        if hh % 2 == 0:
            y_wide = _dot(xn, wk_ref[:, hh * LANES:hh * LANES + wide])
        y = y_wide[:, (hh % 2) * LANES:(hh % 2 + 1) * LANES]
        ms = jnp.sum(y * y, axis=-1, keepdims=True) * (1.0 / HEAD_DIM)
        y = y * lax.rsqrt(ms + EPS) * kg_ref[...]
        partner = jnp.where(first, pltpu.roll(y, LANES - quarter, 1), pltpu.roll(y, quarter, 1))
        k_ref[:, hh * LANES:(hh + 1) * LANES] = (y * cos + partner * sin).astype(BF16)

    fill = FA_VROWS - HEAD_DIM
    ones_then_zeros = jnp.where(lax.broadcasted_iota(jnp.int32, (fill, tm), 0) == 0, 1.0, 0.0).astype(BF16)
    assert GQA_KV_HEADS * HEAD_DIM == wide
    v_wide = _dot(xn, wv_ref[...])
    for pair in range(GQA_KV_HEADS // 2):
        vt = v_wide[:, pair * LANES:(pair + 1) * LANES].T
        for half in range(2):
            base = (2 * pair + half) * FA_VROWS
            vt_ref[0, base:base + HEAD_DIM, :] = vt[half * HEAD_DIM:(half + 1) * HEAD_DIM].astype(BF16)
            vt_ref[0, base + HEAD_DIM:base + FA_VROWS, :] = ones_then_zeros


def gqa_qkv(h, g, wq, wk, wv, q_gain, k_gain):
    t, d = h.shape
    tm = FA_TK
    seq_blocks = SEQ // tm
    wqb = wq.astype(BF16)
    wkp = _pad_heads(wk, GQA_KV_HEADS, LANES).astype(BF16)
    wvb = wv.astype(BF16)
    cos, sin = _rope_tables()
    q_scale = (HEAD_DIM ** -0.5) * float(np.log2(np.e))
    cost, sint = (cos[:, :HEAD_DIM] * q_scale).T, (sin[:, :HEAD_DIM] * q_scale).T
    qg = q_gain.reshape(HEAD_DIM, 1)
    kg = jnp.pad(k_gain, (0, LANES - HEAD_DIM)).reshape(1, LANES)
    const = lambda shape: pl.BlockSpec(shape, lambda i: (0,) * len(shape))
    return pl.pallas_call(
        _gqa_qkv_kernel,
        grid=(t // tm,),
        in_specs=[
            pl.BlockSpec((tm, d), lambda i: (i, 0)),
            const((1, d)),
            const(wqb.shape),
            const(wkp.shape),
            const(wvb.shape),
            const((HEAD_DIM, 1)),
            const((1, LANES)),
            pl.BlockSpec((tm, LANES), lambda i: (i % seq_blocks, 0)),
            pl.BlockSpec((tm, LANES), lambda i: (i % seq_blocks, 0)),
            pl.BlockSpec((HEAD_DIM, tm), lambda i: (0, i % seq_blocks)),
            pl.BlockSpec((HEAD_DIM, tm), lambda i: (0, i % seq_blocks)),
        ],
        out_specs=[
            pl.BlockSpec((1, GQA_Q_HEADS * LANES, tm), lambda i: (i, 0, 0)),
            pl.BlockSpec((tm, GQA_KV_HEADS * LANES), lambda i: (i, 0)),
            pl.BlockSpec((1, GQA_KV_HEADS * FA_VROWS, tm), lambda i: (i, 0, 0)),
        ],
        out_shape=[
            jax.ShapeDtypeStruct((t // tm, GQA_Q_HEADS * LANES, tm), BF16),
            jax.ShapeDtypeStruct((t, GQA_KV_HEADS * LANES), BF16),
            jax.ShapeDtypeStruct((t // tm, GQA_KV_HEADS * FA_VROWS, tm), BF16),
        ],
        compiler_params=pltpu.CompilerParams(
            dimension_semantics=("parallel",), vmem_limit_bytes=VMEM_LIMIT),
        name="gqa_qkv",
    )(h, g.reshape(1, d), wqb, wkp, wvb, qg, kg, cos, sin, cost, sint)


def _flash_kernel(qt_ref, k_ref, vt_ref, o_ref, m_ref, acc_ref, jump_ref):
    n_kv = SEQ // FA_TK

    def scores(j, g):
        kj = k_ref[pl.ds(pl.multiple_of(j * FA_TK, FA_TK), FA_TK), :]
        return _dot(kj, qt_ref[0, g * LANES:(g + 1) * LANES, :])

    def reset():
        m_ref[...] = jnp.full(m_ref.shape, -jnp.inf, F32)
        acc_ref[...] = jnp.zeros_like(acc_ref)

    def step_exact(j, g, s):
        m_prev = m_ref[g]
        m_new = jnp.maximum(m_prev, jnp.max(s, axis=0, keepdims=True))
        p = jnp.exp2(s - m_new)
        acc_ref[g] = jnp.exp2(m_prev - m_new) * acc_ref[g] + _dot(vt_ref[j], p.astype(BF16))
        m_ref[g] = m_new

    def step_lagged(j, g, s):
        m_prev = m_ref[g]
        p = jnp.exp2(s - m_prev)
        block_max = jnp.max(s, axis=0, keepdims=True)
        m_new = jnp.maximum(m_prev, block_max)
        acc_ref[g] = jnp.exp2(m_prev - m_new) * (acc_ref[g] + _dot(vt_ref[j], p.astype(BF16)))
        m_ref[g] = m_new
        jump_ref[g] = jnp.maximum(jump_ref[g], block_max - m_prev)

    def sweep(j, s, step):
        for g in range(GQA_GROUPS):
            if g + 1 < GQA_GROUPS:
                s_next = scores(j, g + 1)
            else:
                s_next = scores(jnp.minimum(j + 1, n_kv - 1), 0)
            step(j, g, s)
            s = s_next
        return s

    acc_ref[...] = jnp.zeros_like(acc_ref)
    jump_ref[...] = jnp.zeros_like(jump_ref)
    k_first = k_ref[0:FA_INIT_KEYS, :]
    for g in range(GQA_GROUPS):
        m_ref[g] = jnp.max(_dot(k_first, qt_ref[0, g * LANES:(g + 1) * LANES, :]), axis=0, keepdims=True)
    lax.fori_loop(0, n_kv, lambda j, s: sweep(j, s, step_lagged), scores(0, 0), unroll=FA_UNROLL)

    @pl.when(jnp.max(jump_ref[...]) > FA_MAX_JUMP)
    def _():
        reset()

        def body(j, carry):
            for g in range(GQA_GROUPS):
                step_exact(j, g, scores(j, g))
            return carry

        lax.fori_loop(0, n_kv, body, 0)

    for pair in range(GQA_GROUPS // 2):
        halves = []
        for g in (2 * pair, 2 * pair + 1):
            acc = acc_ref[g]
            halves.append(acc[:HEAD_DIM] / acc[HEAD_DIM:HEAD_DIM + 1])
        o_ref[:, pair * LANES:(pair + 1) * LANES] = jnp.concatenate(halves, axis=0).T.astype(o_ref.dtype)


def flash_attention(qt, k, vt):
    q_blocks = SEQ // FA_TQ
    q_per_tile = FA_TK // FA_TQ
    kv_blocks = SEQ // FA_TK
    gw = GQA_GROUPS * LANES
    return pl.pallas_call(
        _flash_kernel,
        grid=(BATCH, GQA_KV_HEADS, q_blocks),
        in_specs=[
            pl.BlockSpec((1, gw, FA_TQ), lambda b, kh, i: (b * kv_blocks + i // q_per_tile, kh, i % q_per_tile)),
            pl.BlockSpec((SEQ, LANES), lambda b, kh, i: (b, kh)),
            pl.BlockSpec((kv_blocks, FA_VROWS, FA_TK), lambda b, kh, i: (b, kh, 0)),
        ],
        out_specs=pl.BlockSpec((FA_TQ, GQA_GROUPS * HEAD_DIM), lambda b, kh, i: (b * q_blocks + i, kh)),
        out_shape=jax.ShapeDtypeStruct((TOKENS, D_MODEL), BF16),
        scratch_shapes=[
            pltpu.VMEM((GQA_GROUPS, 1, FA_TQ), F32),
            pltpu.VMEM((GQA_GROUPS, FA_VROWS, FA_TQ), F32),
            pltpu.VMEM((GQA_GROUPS, 1, FA_TQ), F32),
        ],
        compiler_params=pltpu.CompilerParams(
            dimension_semantics=("parallel", "parallel", "parallel"), vmem_limit_bytes=VMEM_LIMIT),
        name="gqa_flash",
    )(qt, k, vt)


def _router_kernel(h_ref, gin_ref, wr_ref, xn_ref, gates_ref, rank_ref, rankt_ref, count_ref):
    tm = h_ref.shape[0]
    lane = lax.broadcasted_iota(jnp.int32, (1, LANES), 1)
    x = _rms(h_ref[...], gin_ref[...])
    xn_ref[...] = x.astype(BF16)
    logits = jnp.dot(x, wr_ref[...], preferred_element_type=F32, precision=lax.Precision.HIGHEST)
    logits = jnp.where(lane < N_EXPERTS, logits, -jnp.inf)
    m1 = jnp.max(logits, axis=-1, keepdims=True)
    i1 = jnp.min(jnp.where(logits == m1, lane, LANES), axis=-1, keepdims=True)
    rest = jnp.where(lane == i1, -jnp.inf, logits)
    m2 = jnp.max(rest, axis=-1, keepdims=True)
    i2 = jnp.min(jnp.where(rest == m2, lane, LANES), axis=-1, keepdims=True)
    e2 = jnp.exp(m2 - m1)
    denom = 1.0 + e2
    gates_ref[...] = jnp.where(lane == i1, 1.0 / denom, 0.0) + jnp.where(lane == i2, e2 / denom, 0.0)
    chosen = jnp.logical_or(lane == i1, lane == i2)
    earlier = lax.broadcasted_iota(jnp.int32, (tm, tm), 1) < lax.broadcasted_iota(jnp.int32, (tm, tm), 0)
    prefix = _dot(jnp.where(earlier, 1.0, 0.0).astype(BF16), jnp.where(chosen, 1.0, 0.0).astype(BF16))
    rank = jnp.where(chosen, prefix, -1.0)
    rank_ref[...] = rank
    rankt_ref[0] = rank.T
    count_ref[0] = jnp.sum(jnp.where(chosen, 1.0, 0.0), axis=0, keepdims=True)


def _moe_kernel(cnt_ref, xn_ref, gates_ref, rank_ref, rankt_ref, wg_ref, wu_ref, wd_ref, h_ref, gout_ref, o_ref,
                xs_ref, yacc_ref, acc_ref):
    i = pl.program_id(0)
    e = pl.program_id(1)
    j = pl.program_id(2)
    last_e = pl.num_programs(1) - 1
    last_j = pl.num_programs(2) - 1
    n_rows = cnt_ref[i * N_EXPERTS + e]

    @pl.when(jnp.logical_and(e == 0, j == 0))
    def _():
        acc_ref[...] = jnp.zeros_like(acc_ref)

    def run(group, n_groups):
        def over_groups(fn):
            if isinstance(n_groups, int):
                for c in range(n_groups):
                    fn(c * group)
            else:
                def body(c, carry):
                    fn(pl.multiple_of(c * group, LANES))
                    return carry

                lax.fori_loop(0, n_groups, body, 0)

        @pl.when(j == 0)
        def _():
            rank_row = rankt_ref[0, pl.ds(e, 1), :]

            def compact(r0):
                slot = (lax.broadcasted_iota(jnp.int32, (group, 1), 0) + r0).astype(F32)
                onehot = jnp.where(rank_row == slot, 1.0, 0.0).astype(BF16)
                xs_ref[pl.ds(r0, group), :] = _dot(onehot, xn_ref[...]).astype(BF16)
                yacc_ref[pl.ds(r0, group), :] = jnp.zeros((group, yacc_ref.shape[1]), F32)

            over_groups(compact)

        def expert(r0):
            xs = xs_ref[pl.ds(r0, group), :]
            gate = _dot(xs, wg_ref[0])
            up = _dot(xs, wu_ref[0])
            act = (gate * jax.nn.sigmoid(gate) * up).astype(BF16)
            yacc_ref[pl.ds(r0, group), :] += _dot(act, wd_ref[0])

        over_groups(expert)

        @pl.when(j == last_j)
        def _():
            lane = lax.broadcasted_iota(jnp.int32, (1, LANES), 1)
            rank_col = jnp.sum(jnp.where(lane == e, rank_ref[...], 0.0), axis=-1, keepdims=True)
            gate_col = jnp.sum(jnp.where(lane == e, gates_ref[...], 0.0), axis=-1, keepdims=True)

            def expand(r0):
                slot = (lax.broadcasted_iota(jnp.int32, (1, group), 1) + r0).astype(F32)
                onehot = jnp.where(rank_col == slot, 1.0, 0.0).astype(BF16)
                acc_ref[...] += gate_col * _dot(onehot, yacc_ref[pl.ds(r0, group), :].astype(BF16))

            over_groups(expand)

    @pl.when(jnp.logical_and(n_rows > 0, n_rows <= MOE_GROUP_SMALL))
    def _():
        run(MOE_GROUP_SMALL, 1)

    @pl.when(jnp.logical_and(n_rows > MOE_GROUP_SMALL, n_rows <= MOE_GROUP))
    def _():
        run(MOE_GROUP, 1)

    @pl.when(n_rows > MOE_GROUP)
    def _():
        run(MOE_GROUP, (n_rows + (MOE_GROUP - 1)) // MOE_GROUP)

    @pl.when(jnp.logical_and(e == last_e, j == last_j))
    def _():
        o_ref[...] = h_ref[...] + _rms(acc_ref[...], gout_ref[...])


def moe_block(h, g_in, w_router, wg, wu, wd, g_out, tm=MOE_TM, tf=MOE_TF):
    t, d = h.shape
    n_e, _, f = wg.shape
    n_tiles = t // tm
    wr = jnp.pad(w_router, ((0, 0), (0, LANES - n_e)))
    xn, gates, rank, rankt, counts = pl.pallas_call(
        _router_kernel,
        grid=(n_tiles,),
        in_specs=[
            pl.BlockSpec((tm, d), lambda i: (i, 0)),
            pl.BlockSpec((1, d), lambda i: (0, 0)),
            pl.BlockSpec((d, LANES), lambda i: (0, 0)),
        ],
        out_specs=[
            pl.BlockSpec((tm, d), lambda i: (i, 0)),
            pl.BlockSpec((tm, LANES), lambda i: (i, 0)),
            pl.BlockSpec((tm, LANES), lambda i: (i, 0)),
            pl.BlockSpec((1, LANES, tm), lambda i: (i, 0, 0)),
            pl.BlockSpec((1, 1, LANES), lambda i: (i, 0, 0)),
        ],
        out_shape=[
            jax.ShapeDtypeStruct((t, d), BF16),
            jax.ShapeDtypeStruct((t, LANES), F32),
            jax.ShapeDtypeStruct((t, LANES), F32),
            jax.ShapeDtypeStruct((n_tiles, LANES, tm), F32),
            jax.ShapeDtypeStruct((n_tiles, 1, LANES), F32),
        ],
        compiler_params=pltpu.CompilerParams(
            dimension_semantics=("parallel",), vmem_limit_bytes=VMEM_LIMIT),
        name="moe_router",
    )(h, g_in.reshape(1, d), wr)
    n_rows = counts[:, 0, :n_e].astype(jnp.int32)
    max_rows = -(-tm // MOE_GROUP) * MOE_GROUP
    grid_spec = pltpu.PrefetchScalarGridSpec(
        num_scalar_prefetch=1,
        grid=(n_tiles, n_e, f // tf),
        in_specs=[
            pl.BlockSpec((tm, d), lambda i, e, j, ng: (i, 0)),
            pl.BlockSpec((tm, LANES), lambda i, e, j, ng: (i, 0)),
            pl.BlockSpec((tm, LANES), lambda i, e, j, ng: (i, 0)),
            pl.BlockSpec((1, LANES, tm), lambda i, e, j, ng: (i, 0, 0)),
            pl.BlockSpec((1, d, tf), lambda i, e, j, ng: (e, 0, j)),
            pl.BlockSpec((1, d, tf), lambda i, e, j, ng: (e, 0, j)),
            pl.BlockSpec((1, tf, d), lambda i, e, j, ng: (e, j, 0)),
            pl.BlockSpec((tm, d), lambda i, e, j, ng: (i, 0)),
            pl.BlockSpec((1, d), lambda i, e, j, ng: (0, 0)),
        ],
        out_specs=pl.BlockSpec((tm, d), lambda i, e, j, ng: (i, 0)),
        scratch_shapes=[
            pltpu.VMEM((max_rows, d), BF16),
            pltpu.VMEM((max_rows, d), F32),
            pltpu.VMEM((tm, d), F32),
        ],
    )
    return pl.pallas_call(
        _moe_kernel,
        grid_spec=grid_spec,
        out_shape=jax.ShapeDtypeStruct((t, d), F32),
        compiler_params=pltpu.CompilerParams(
            dimension_semantics=("parallel", "arbitrary", "arbitrary"), vmem_limit_bytes=VMEM_LIMIT),
        name="moe_swiglu",
    )(n_rows.reshape(-1), xn, gates, rank, rankt, wg, wu, wd, h, g_out.reshape(1, d))


def kernel(x, norm_g, na_w_qkv, na_rpb, na_w_o, gqa_w_qkv, gqa_q_norm, gqa_k_norm, gqa_w_o,
           ffn_w_gate, ffn_w_up, ffn_w_down, moe_w_router, moe_w_gate, moe_w_up, moe_w_down):
    assert x.shape == (BATCH, SEQ, D_MODEL)
    h = x.reshape(TOKENS, D_MODEL)

    log2e = float(np.log2(np.e))
    qt0, k0, vt0 = na_qkv(h, norm_g[0, 0], na_w_qkv[0].astype(BF16), (HEAD_DIM ** -0.5) * log2e)
    a = neighborhood_attention(qt0, k0, vt0, _na_bias_tables(na_rpb[0] * log2e))
    h = proj_norm_res(a, na_w_o[0].astype(BF16), norm_g[0, 1], h)
    h = ffn_block(h, norm_g[0, 2], ffn_w_gate[0].astype(BF16), ffn_w_up[0].astype(BF16),
                  ffn_w_down[0].astype(BF16), norm_g[0, 3])

    nq = GQA_Q_HEADS * HEAD_DIM
    nkv = GQA_KV_HEADS * HEAD_DIM
    wq, wk, wv = gqa_w_qkv[0][:, :nq], gqa_w_qkv[0][:, nq:nq + nkv], gqa_w_qkv[0][:, nq + nkv:]
    qt, k, vt = gqa_qkv(h, norm_g[1, 0], wq, wk, wv, gqa_q_norm[0], gqa_k_norm[0])
    o = flash_attention(qt, k, vt)
    h = proj_norm_res(o, gqa_w_o[0].astype(BF16), norm_g[1, 1], h)
    h = moe_block(h, norm_g[1, 2], moe_w_router[0], moe_w_gate[0].astype(BF16), moe_w_up[0].astype(BF16),
                  moe_w_down[0].astype(BF16), norm_g[1, 3])
    return h.reshape(BATCH, SEQ, D_MODEL)
```

```python
import functools

import jax
import jax.numpy as jnp
import numpy as np
from jax import lax
from jax.experimental import pallas as pl
from jax.experimental.pallas import tpu as pltpu

F32 = jnp.float32
BF16 = jnp.bfloat16

D_MODEL = 1024
BATCH = 2
SEQ = 8192
TOKENS = BATCH * SEQ
GRID_W = 64
GRID_H = SEQ // GRID_W
NA_HEADS = 16
HEAD_DIM = 64
WIN_H = 8
WIN_W = 16
GQA_Q_HEADS = 16
GQA_KV_HEADS = 4
GQA_GROUPS = GQA_Q_HEADS // GQA_KV_HEADS
ROPE_THETA = 10000.0
D_FF = 3584
N_EXPERTS = 8
EPS = 1e-6
LANES = 128
NEG_BIG = -1e30

VMEM_LIMIT = 56 * 1024 * 1024

NA_QROWS = 4
NA_QTOK = NA_QROWS * GRID_W
NA_BAND_BLOCKS = 3
NA_KTOK = NA_BAND_BLOCKS * NA_QTOK
NA_ROW_BLOCKS = GRID_H // NA_QROWS
NA_HEADS_PER_STEP = 8

FA_TQ = 512
FA_TK = 512
MOE_TM = 1024
MOE_TF = 896
MOE_GROUP = 384
MOE_GROUP_SMALL = 256

FA_UNROLL = 8
FA_INIT_KEYS = 16
FA_MAX_JUMP = 24.0
FA_VROWS = 80


def _rms(x, g):
    ms = jnp.mean(x * x, axis=-1, keepdims=True)
    return x * lax.rsqrt(ms + EPS) * g


def _dot(a, b):
    return jnp.dot(a, b, preferred_element_type=F32)


def _dot_nt(a, b):
    return lax.dot_general(a, b, (((1,), (1,)), ((), ())), preferred_element_type=F32)


def _na_qkv_kernel(x_ref, g_ref, w_ref, qt_ref, k_ref, vt_ref, *, q_scale, n_chunk):
    xn = _rms(x_ref[...], g_ref[...]).astype(BF16)
    tm = xn.shape[0]
    d = k_ref.shape[1]

    def store_transposed(dst_ref, y, col0):
        for t0 in range(0, tm, NA_QTOK):
            for c0 in range(0, n_chunk, LANES):
                dst_ref[t0 // NA_QTOK, col0 + c0:col0 + c0 + LANES, :] = (
                    y[t0:t0 + NA_QTOK, c0:c0 + LANES].T.astype(BF16))

    for n0 in range(0, d, n_chunk):
        store_transposed(qt_ref, _dot(xn, w_ref[:, n0:n0 + n_chunk]) * q_scale, n0)
        k_ref[:, n0:n0 + n_chunk] = _dot(xn, w_ref[:, d + n0:d + n0 + n_chunk]).astype(BF16)
        store_transposed(vt_ref, _dot(xn, w_ref[:, 2 * d + n0:2 * d + n0 + n_chunk]), n0)


def na_qkv(x, g, w, q_scale, tm=512, n_chunk=512):
    t, d = x.shape
    n = w.shape[1]
    tiles = tm // NA_QTOK
    transposed = jax.ShapeDtypeStruct((t // NA_QTOK, d, NA_QTOK), BF16)
    return pl.pallas_call(
        functools.partial(_na_qkv_kernel, q_scale=q_scale, n_chunk=n_chunk),
        grid=(t // tm,),
        in_specs=[
            pl.BlockSpec((tm, d), lambda i: (i, 0)),
            pl.BlockSpec((1, d), lambda i: (0, 0)),
            pl.BlockSpec((d, n), lambda i: (0, 0)),
        ],
        out_specs=[
            pl.BlockSpec((tiles, d, NA_QTOK), lambda i: (i, 0, 0)),
            pl.BlockSpec((tm, d), lambda i: (i, 0)),
            pl.BlockSpec((tiles, d, NA_QTOK), lambda i: (i, 0, 0)),
        ],
        out_shape=[transposed, jax.ShapeDtypeStruct((t, d), BF16), transposed],
        compiler_params=pltpu.CompilerParams(
            dimension_semantics=("parallel",), vmem_limit_bytes=VMEM_LIMIT),
        name="na_qkv",
    )(x, g.reshape(1, d), w)


def _proj_norm_res_kernel(a_ref, w_ref, g_ref, h_ref, o_ref):
    y = _dot(a_ref[...], w_ref[...])
    o_ref[...] = h_ref[...] + _rms(y, g_ref[...])


def proj_norm_res(a, w, g, h, tm=512):
    t, k = a.shape
    d = w.shape[1]
    return pl.pallas_call(
        _proj_norm_res_kernel,
        grid=(t // tm,),
        in_specs=[
            pl.BlockSpec((tm, k), lambda i: (i, 0)),
            pl.BlockSpec((k, d), lambda i: (0, 0)),
            pl.BlockSpec((1, d), lambda i: (0, 0)),
            pl.BlockSpec((tm, d), lambda i: (i, 0)),
        ],
        out_specs=pl.BlockSpec((tm, d), lambda i: (i, 0)),
        out_shape=jax.ShapeDtypeStruct((t, d), F32),
        compiler_params=pltpu.CompilerParams(
            dimension_semantics=("parallel",), vmem_limit_bytes=VMEM_LIMIT),
        name="proj_norm_res",
    )(a, w, g.reshape(1, d), h)


def _na_bias_expand_kernel(plane_ref, cols_ref, o_ref):
    case = pl.program_id(0)
    band_rows = NA_BAND_BLOCKS * NA_QROWS
    for i in range(NA_QROWS):
        for m in range(band_rows):
            u = plane_ref[(case * NA_QROWS + i) * band_rows + m]
            o_ref[0, 0, m * GRID_W:(m + 1) * GRID_W, i * GRID_W:(i + 1) * GRID_W] = cols_ref[0, u]


def _na_bias_tables(rpb):
    i = np.arange(NA_QROWS)[:, None]
    m = np.arange(NA_BAND_BLOCKS * NA_QROWS)[None, :]
    ridx, rvalid = [], []
    for rb in (0, 1, NA_ROW_BLOCKS - 1):
        band0 = NA_QROWS * min(max(rb - 1, 0), NA_ROW_BLOCKS - NA_BAND_BLOCKS)
        r = NA_QROWS * rb + i
        kr = band0 + m
        rs = np.clip(r - WIN_H // 2, 0, GRID_H - WIN_H)
        rvalid.append((kr >= rs) & (kr < rs + WIN_H))
        ridx.append(np.clip(kr - r + WIN_H - 1, 0, 2 * WIN_H - 2))
    n_dr = 2 * WIN_H - 1
    plane = np.where(np.stack(rvalid), np.stack(ridx), n_dr)
    c = np.arange(GRID_W)[:, None]
    kc = np.arange(GRID_W)[None, :]
    cs = np.clip(c - WIN_W // 2, 0, GRID_W - WIN_W)
    cvalid = (kc >= cs) & (kc < cs + WIN_W)
    valid_t = cvalid.T
    pick = ((kc.T - c.T + WIN_W - 1)[None] == np.arange(2 * WIN_W - 1)[:, None, None]) & valid_t[None]
    cols_t = jnp.einsum('hud,dkc->hukc', rpb, jnp.asarray(pick, F32), precision=lax.Precision.HIGHEST)
    cols_t = jnp.where(valid_t[None, None], cols_t, NEG_BIG)
    cols_t = jnp.concatenate([cols_t, jnp.full_like(cols_t[:, :1], NEG_BIG)], axis=1)
    n_cases = plane.shape[0]
    grid_spec = pltpu.PrefetchScalarGridSpec(
        num_scalar_prefetch=1,
        grid=(n_cases, NA_HEADS),
        in_specs=[pl.BlockSpec((1, n_dr + 1, GRID_W, GRID_W), lambda z, h, pr: (h, 0, 0, 0))],
        out_specs=pl.BlockSpec((1, 1, NA_KTOK, NA_QTOK), lambda z, h, pr: (z, h // 2, 0, h % 2)),
    )
    return pl.pallas_call(
        _na_bias_expand_kernel,
        grid_spec=grid_spec,
        out_shape=jax.ShapeDtypeStruct((n_cases, NA_HEADS // 2, NA_KTOK, 2 * NA_QTOK), F32),
        compiler_params=pltpu.CompilerParams(dimension_semantics=("parallel", "parallel")),
        name="na_bias_expand",
    )(jnp.asarray(plane.reshape(-1), jnp.int32), cols_t)


def _na_kernel(qt_ref, k0_ref, k1_ref, k2_ref, vt0_ref, vt1_ref, vt2_ref, bias_ref, o_ref):
    row = lax.broadcasted_iota(jnp.int32, (LANES, 1), 0)
    low = row < HEAD_DIM

    n_pairs = NA_HEADS_PER_STEP // 2

    def scores(pair):
        rows = slice(pair * LANES, (pair + 1) * LANES)
        qt = qt_ref[0, rows, :]
        k = jnp.concatenate([k0_ref[:, rows], k1_ref[:, rows], k2_ref[:, rows]], axis=0)
        zero = jnp.zeros_like(qt)
        qt2 = jnp.concatenate([jnp.where(low, qt, zero), jnp.where(low, zero, qt)], axis=1)
        return _dot(k, qt2)

    s = scores(0)
    for pair in range(n_pairs):
        s_next = scores(pair + 1) if pair + 1 < n_pairs else None
        rows = slice(pair * LANES, (pair + 1) * LANES)
        vt = jnp.concatenate([vt0_ref[0, rows, :], vt1_ref[0, rows, :], vt2_ref[0, rows, :]], axis=1)
        one = jnp.ones_like(vt)
        s = s + bias_ref[0, pair]
        p = jnp.exp2(s - jnp.max(s, axis=0, keepdims=True)).astype(BF16)
        pv_a = _dot(jnp.where(low, vt, one), p[:, :NA_QTOK])
        pv_b = _dot(jnp.where(low, one, vt), p[:, NA_QTOK:])
        both = jnp.concatenate([pv_a[:HEAD_DIM] / pv_a[HEAD_DIM:], pv_b[HEAD_DIM:] / pv_b[:HEAD_DIM]], axis=0)
        o_ref[:, rows] = both.T.astype(o_ref.dtype)
        s = s_next


def neighborhood_attention(qt, k, vt, bias_tab):
    hw = NA_HEADS_PER_STEP * HEAD_DIM
    nh = D_MODEL // hw
    blocks_per_seq = SEQ // NA_QTOK

    def band(rb):
        return jnp.clip(rb - 1, 0, NA_ROW_BLOCKS - NA_BAND_BLOCKS)

    def case(rb):
        return jnp.where(rb == 0, 0, jnp.where(rb == NA_ROW_BLOCKS - 1, 2, 1))

    qt_spec = pl.BlockSpec((1, hw, NA_QTOK), lambda h, b, rb: (b * blocks_per_seq + rb, h, 0))
    k_specs = [pl.BlockSpec((NA_QTOK, hw), lambda h, b, rb, j=j: (b * blocks_per_seq + band(rb) + j, h))
               for j in range(NA_BAND_BLOCKS)]
    vt_specs = [pl.BlockSpec((1, hw, NA_QTOK), lambda h, b, rb, j=j: (b * blocks_per_seq + band(rb) + j, h, 0))
                for j in range(NA_BAND_BLOCKS)]
    bias_spec = pl.BlockSpec((1, NA_HEADS_PER_STEP // 2, NA_KTOK, 2 * NA_QTOK),
                             lambda h, b, rb: (case(rb), h, 0, 0))
    return pl.pallas_call(
        _na_kernel,
        grid=(nh, BATCH, NA_ROW_BLOCKS),
        in_specs=[qt_spec] + k_specs + vt_specs + [bias_spec],
        out_specs=pl.BlockSpec((NA_QTOK, hw), lambda h, b, rb: (b * blocks_per_seq + rb, h)),
        out_shape=jax.ShapeDtypeStruct((TOKENS, D_MODEL), BF16),
        compiler_params=pltpu.CompilerParams(
            dimension_semantics=("parallel", "parallel", "parallel"), vmem_limit_bytes=VMEM_LIMIT),
        name="na_attention",
    )(qt, k, k, k, vt, vt, vt, bias_tab)


def _ffn_kernel(h_ref, gin_ref, wg_ref, wu_ref, wd_ref, gout_ref, o_ref, xn_ref, acc_ref):
    j = pl.program_id(1)

    @pl.when(j == 0)
    def _():
        xn_ref[...] = _rms(h_ref[...], gin_ref[...]).astype(BF16)
        acc_ref[...] = jnp.zeros_like(acc_ref)

    xn = xn_ref[...]
    gate = _dot(xn, wg_ref[...])
    up = _dot(xn, wu_ref[...])
    act = (gate * jax.nn.sigmoid(gate) * up).astype(BF16)
    acc_ref[...] += _dot(act, wd_ref[...])

    @pl.when(j == pl.num_programs(1) - 1)
    def _():
        o_ref[...] = h_ref[...] + _rms(acc_ref[...], gout_ref[...])


def ffn_block(h, g_in, wg, wu, wd, g_out, tm=1024, tf=512):
    t, d = h.shape
    f = wg.shape[1]
    return pl.pallas_call(
        _ffn_kernel,
        grid=(t // tm, f // tf),
        in_specs=[
            pl.BlockSpec((tm, d), lambda i, j: (i, 0)),
            pl.BlockSpec((1, d), lambda i, j: (0, 0)),
            pl.BlockSpec((d, tf), lambda i, j: (0, j)),
            pl.BlockSpec((d, tf), lambda i, j: (0, j)),
            pl.BlockSpec((tf, d), lambda i, j: (j, 0)),
            pl.BlockSpec((1, d), lambda i, j: (0, 0)),
        ],
        out_specs=pl.BlockSpec((tm, d), lambda i, j: (i, 0)),
        out_shape=jax.ShapeDtypeStruct((t, d), F32),
        scratch_shapes=[pltpu.VMEM((tm, d), BF16), pltpu.VMEM((tm, d), F32)],
        compiler_params=pltpu.CompilerParams(
            dimension_semantics=("parallel", "arbitrary"), vmem_limit_bytes=VMEM_LIMIT),
        name="ffn_swiglu",
    )(h, g_in.reshape(1, d), wg, wu, wd, g_out.reshape(1, d))


def _pad_heads(w, n_heads, width):
    d = w.shape[0]
    w = w.reshape(d, n_heads, HEAD_DIM)
    return jnp.pad(w, ((0, 0), (0, 0), (0, width - HEAD_DIM))).reshape(d, n_heads * width)


def _rope_tables():
    quarter = HEAD_DIM // 4
    freqs = ROPE_THETA ** (-jnp.arange(quarter, dtype=F32) / quarter)
    t = jnp.arange(SEQ)
    row_ang = (t // GRID_W).astype(F32)[:, None] * freqs[None, :]
    col_ang = (t % GRID_W).astype(F32)[:, None] * freqs[None, :]
    zeros = jnp.zeros((SEQ, LANES - HEAD_DIM), F32)
    cos = jnp.concatenate([jnp.cos(row_ang), jnp.cos(row_ang), jnp.cos(col_ang), jnp.cos(col_ang), zeros], axis=1)
    sin = jnp.concatenate([-jnp.sin(row_ang), jnp.sin(row_ang), -jnp.sin(col_ang), jnp.sin(col_ang), zeros], axis=1)
    return cos, sin


def _gqa_qkv_kernel(x_ref, g_ref, wq_ref, wk_ref, wv_ref, qg_ref, kg_ref, cos_ref, sin_ref, cost_ref, sint_ref,
                    qt_ref, k_ref, vt_ref):
    xn = _rms(x_ref[...], g_ref[...]).astype(BF16)
    tm = xn.shape[0]
    quarter = HEAD_DIM // 4

    cost = cost_ref[...]
    sint = sint_ref[...]
    qg = qg_ref[...]
    zero_rows = jnp.zeros((LANES - HEAD_DIM, tm), BF16)
    wide = 2 * LANES
    for pair in range(GQA_Q_HEADS // 2):
        if pair % 2 == 0:
            y_wide = _dot(xn, wq_ref[:, pair * LANES:pair * LANES + wide])
        yt = y_wide[:, (pair % 2) * LANES:(pair % 2 + 1) * LANES].T
        for half in range(2):
            y = yt[half * HEAD_DIM:(half + 1) * HEAD_DIM]
            ms = jnp.sum(y * y, axis=0, keepdims=True) * (1.0 / HEAD_DIM)
            y = y * lax.rsqrt(ms + EPS) * qg
            partner = jnp.concatenate(
                [y[quarter:2 * quarter], y[:quarter], y[3 * quarter:], y[2 * quarter:3 * quarter]], axis=0)
            base = (2 * pair + half) * LANES
            qt_ref[0, base:base + HEAD_DIM, :] = (y * cost + partner * sint).astype(BF16)
            qt_ref[0, base + HEAD_DIM:base + LANES, :] = zero_rows

    lane = lax.broadcasted_iota(jnp.int32, (1, LANES), 1)
    first = (lane % (2 * quarter)) < quarter
    cos = cos_ref[...]
    sin = sin_ref[...]
    for hh in range(GQA_KV_HEADS):
        if hh % 2 == 0:
            y_wide = _dot(xn, wk_ref[:, hh * LANES:hh * LANES + wide])
        y = y_wide[:, (hh % 2) * LANES:(hh % 2 + 1) * LANES]
        ms = jnp.sum(y * y, axis=-1, keepdims=True) * (1.0 / HEAD_DIM)
        y = y * lax.rsqrt(ms + EPS) * kg_ref[...]
        partner = jnp.where(first, pltpu.roll(y, LANES - quarter, 1), pltpu.roll(y, quarter, 1))
        k_ref[:, hh * LANES:(hh + 1) * LANES] = (y * cos + partner * sin).astype(BF16)

    fill = FA_VROWS - HEAD_DIM
    ones_then_zeros = jnp.where(lax.broadcasted_iota(jnp.int32, (fill, tm), 0) == 0, 1.0, 0.0).astype(BF16)
    assert GQA_KV_HEADS * HEAD_DIM == wide
    v_wide = _dot(xn, wv_ref[...])
    for pair in range(GQA_KV_HEADS // 2):
        vt = v_wide[:, pair * LANES:(pair + 1) * LANES].T
        for half in range(2):
            base = (2 * pair + half) * FA_VROWS
            vt_ref[0, base:base + HEAD_DIM, :] = vt[half * HEAD_DIM:(half + 1) * HEAD_DIM].astype(BF16)
            vt_ref[0, base + HEAD_DIM:base + FA_VROWS, :] = ones_then_zeros


def gqa_qkv(h, g, wq, wk, wv, q_gain, k_gain):
    t, d = h.shape
    tm = FA_TK
    seq_blocks = SEQ // tm
    wqb = wq.astype(BF16)
    wkp = _pad_heads(wk, GQA_KV_HEADS, LANES).astype(BF16)
    wvb = wv.astype(BF16)
    cos, sin = _rope_tables()
    q_scale = (HEAD_DIM ** -0.5) * float(np.log2(np.e))
    cost, sint = (cos[:, :HEAD_DIM] * q_scale).T, (sin[:, :HEAD_DIM] * q_scale).T
    qg = q_gain.reshape(HEAD_DIM, 1)
    kg = jnp.pad(k_gain, (0, LANES - HEAD_DIM)).reshape(1, LANES)
    const = lambda shape: pl.BlockSpec(shape, lambda i: (0,) * len(shape))
    return pl.pallas_call(
        _gqa_qkv_kernel,
        grid=(t // tm,),
        in_specs=[
            pl.BlockSpec((tm, d), lambda i: (i, 0)),
            const((1, d)),
            const(wqb.shape),
            const(wkp.shape),
            const(wvb.shape),
            const((HEAD_DIM, 1)),
            const((1, LANES)),
            pl.BlockSpec((tm, LANES), lambda i: (i % seq_blocks, 0)),
            pl.BlockSpec((tm, LANES), lambda i: (i % seq_blocks, 0)),
            pl.BlockSpec((HEAD_DIM, tm), lambda i: (0, i % seq_blocks)),
            pl.BlockSpec((HEAD_DIM, tm), lambda i: (0, i % seq_blocks)),
        ],
        out_specs=[
            pl.BlockSpec((1, GQA_Q_HEADS * LANES, tm), lambda i: (i, 0, 0)),
            pl.BlockSpec((tm, GQA_KV_HEADS * LANES), lambda i: (i, 0)),
            pl.BlockSpec((1, GQA_KV_HEADS * FA_VROWS, tm), lambda i: (i, 0, 0)),
        ],
        out_shape=[
            jax.ShapeDtypeStruct((t // tm, GQA_Q_HEADS * LANES, tm), BF16),
            jax.ShapeDtypeStruct((t, GQA_KV_HEADS * LANES), BF16),
            jax.ShapeDtypeStruct((t // tm, GQA_KV_HEADS * FA_VROWS, tm), BF16),
        ],
        compiler_params=pltpu.CompilerParams(
            dimension_semantics=("parallel",), vmem_limit_bytes=VMEM_LIMIT),
        name="gqa_qkv",
    )(h, g.reshape(1, d), wqb, wkp, wvb, qg, kg, cos, sin, cost, sint)


def _flash_kernel(qt_ref, k_ref, vt_ref, o_ref, m_ref, acc_ref, jump_ref):
    n_kv = SEQ // FA_TK

    def scores(j, g):
        kj = k_ref[pl.ds(pl.multiple_of(j * FA_TK, FA_TK), FA_TK), :]
        return _dot(kj, qt_ref[0, g * LANES:(g + 1) * LANES, :])

    def reset():
        m_ref[...] = jnp.full(m_ref.shape, -jnp.inf, F32)
        acc_ref[...] = jnp.zeros_like(acc_ref)

    def step_exact(j, g, s):
        m_prev = m_ref[g]
        m_new = jnp.maximum(m_prev, jnp.max(s, axis=0, keepdims=True))
        p = jnp.exp2(s - m_new)
        acc_ref[g] = jnp.exp2(m_prev - m_new) * acc_ref[g] + _dot(vt_ref[j], p.astype(BF16))
        m_ref[g] = m_new

    def step_lagged(j, g, s):
        m_prev = m_ref[g]
        p = jnp.exp2(s - m_prev)
        block_max = jnp.max(s, axis=0, keepdims=True)
        m_new = jnp.maximum(m_prev, block_max)
        acc_ref[g] = jnp.exp2(m_prev - m_new) * (acc_ref[g] + _dot(vt_ref[j], p.astype(BF16)))
        m_ref[g] = m_new
        jump_ref[g] = jnp.maximum(jump_ref[g], block_max - m_prev)

    def sweep(j, s, step):
        for g in range(GQA_GROUPS):
            if g + 1 < GQA_GROUPS:
                s_next = scores(j, g + 1)
            else:
                s_next = scores(jnp.minimum(j + 1, n_kv - 1), 0)
            step(j, g, s)
            s = s_next
        return s

    acc_ref[...] = jnp.zeros_like(acc_ref)
    jump_ref[...] = jnp.zeros_like(jump_ref)
    k_first = k_ref[0:FA_INIT_KEYS, :]
    for g in range(GQA_GROUPS):
        m_ref[g] = jnp.max(_dot(k_first, qt_ref[0, g * LANES:(g + 1) * LANES, :]), axis=0, keepdims=True)
    lax.fori_loop(0, n_kv, lambda j, s: sweep(j, s, step_lagged), scores(0, 0), unroll=FA_UNROLL)

    @pl.when(jnp.max(jump_ref[...]) > FA_MAX_JUMP)
    def _():
        reset()

        def body(j, carry):
            for g in range(GQA_GROUPS):
                step_exact(j, g, scores(j, g))
            return carry

        lax.fori_loop(0, n_kv, body, 0)

    for pair in range(GQA_GROUPS // 2):
        halves = []
        for g in (2 * pair, 2 * pair + 1):
            acc = acc_ref[g]
            halves.append(acc[:HEAD_DIM] / acc[HEAD_DIM:HEAD_DIM + 1])
        o_ref[:, pair * LANES:(pair + 1) * LANES] = jnp.concatenate(halves, axis=0).T.astype(o_ref.dtype)


def flash_attention(qt, k, vt):
    q_blocks = SEQ // FA_TQ
    q_per_tile = FA_TK // FA_TQ
    kv_blocks = SEQ // FA_TK
    gw = GQA_GROUPS * LANES
    return pl.pallas_call(
        _flash_kernel,
        grid=(BATCH, GQA_KV_HEADS, q_blocks),
        in_specs=[
            pl.BlockSpec((1, gw, FA_TQ), lambda b, kh, i: (b * kv_blocks + i // q_per_tile, kh, i % q_per_tile)),
            pl.BlockSpec((SEQ, LANES), lambda b, kh, i: (b, kh)),
            pl.BlockSpec((kv_blocks, FA_VROWS, FA_TK), lambda b, kh, i: (b, kh, 0)),
        ],
        out_specs=pl.BlockSpec((FA_TQ, GQA_GROUPS * HEAD_DIM), lambda b, kh, i: (b * q_blocks + i, kh)),
        out_shape=jax.ShapeDtypeStruct((TOKENS, D_MODEL), BF16),
        scratch_shapes=[
            pltpu.VMEM((GQA_GROUPS, 1, FA_TQ), F32),
            pltpu.VMEM((GQA_GROUPS, FA_VROWS, FA_TQ), F32),
            pltpu.VMEM((GQA_GROUPS, 1, FA_TQ), F32),
        ],
        compiler_params=pltpu.CompilerParams(
            dimension_semantics=("parallel", "parallel", "parallel"), vmem_limit_bytes=VMEM_LIMIT),
        name="gqa_flash",
    )(qt, k, vt)


def _router_kernel(h_ref, gin_ref, wr_ref, xn_ref, gates_ref, rank_ref, rankt_ref, count_ref):
    tm = h_ref.shape[0]
    lane = lax.broadcasted_iota(jnp.int32, (1, LANES), 1)
    x = _rms(h_ref[...], gin_ref[...])
    xn_ref[...] = x.astype(BF16)
    logits = jnp.dot(x, wr_ref[...], preferred_element_type=F32, precision=lax.Precision.HIGHEST)
    logits = jnp.where(lane < N_EXPERTS, logits, -jnp.inf)
    m1 = jnp.max(logits, axis=-1, keepdims=True)
    i1 = jnp.min(jnp.where(logits == m1, lane, LANES), axis=-1, keepdims=True)
    rest = jnp.where(lane == i1, -jnp.inf, logits)
    m2 = jnp.max(rest, axis=-1, keepdims=True)
    i2 = jnp.min(jnp.where(rest == m2, lane, LANES), axis=-1, keepdims=True)
    e2 = jnp.exp(m2 - m1)
    denom = 1.0 + e2
    gates_ref[...] = jnp.where(lane == i1, 1.0 / denom, 0.0) + jnp.where(lane == i2, e2 / denom, 0.0)
    chosen = jnp.logical_or(lane == i1, lane == i2)
    earlier = lax.broadcasted_iota(jnp.int32, (tm, tm), 1) < lax.broadcasted_iota(jnp.int32, (tm, tm), 0)
    prefix = _dot(jnp.where(earlier, 1.0, 0.0).astype(BF16), jnp.where(chosen, 1.0, 0.0).astype(BF16))
    rank = jnp.where(chosen, prefix, -1.0)
    rank_ref[...] = rank
    rankt_ref[0] = rank.T
    count_ref[0] = jnp.sum(jnp.where(chosen, 1.0, 0.0), axis=0, keepdims=True)


def _moe_kernel(cnt_ref, xn_ref, gates_ref, rank_ref, rankt_ref, wg_ref, wu_ref, wd_ref, h_ref, gout_ref, o_ref,
                xs_ref, yacc_ref, acc_ref):
    i = pl.program_id(0)
    e = pl.program_id(1)
    j = pl.program_id(2)
    last_e = pl.num_programs(1) - 1
    last_j = pl.num_programs(2) - 1
    n_rows = cnt_ref[i * N_EXPERTS + e]

    @pl.when(jnp.logical_and(e == 0, j == 0))
    def _():
        acc_ref[...] = jnp.zeros_like(acc_ref)

    def run(group, n_groups):
        def over_groups(fn):
            if isinstance(n_groups, int):
                for c in range(n_groups):
                    fn(c * group)
            else:
                def body(c, carry):
                    fn(pl.multiple_of(c * group, LANES))
                    return carry

                lax.fori_loop(0, n_groups, body, 0)

        @pl.when(j == 0)
        def _():
            rank_row = rankt_ref[0, pl.ds(e, 1), :]

            def compact(r0):
                slot = (lax.broadcasted_iota(jnp.int32, (group, 1), 0) + r0).astype(F32)
                onehot = jnp.where(rank_row == slot, 1.0, 0.0).astype(BF16)
                xs_ref[pl.ds(r0, group), :] = _dot(onehot, xn_ref[...]).astype(BF16)
                yacc_ref[pl.ds(r0, group), :] = jnp.zeros((group, yacc_ref.shape[1]), F32)

            over_groups(compact)

        def expert(r0):
            xs = xs_ref[pl.ds(r0, group), :]
            gate = _dot(xs, wg_ref[0])
            up = _dot(xs, wu_ref[0])
            act = (gate * jax.nn.sigmoid(gate) * up).astype(BF16)
            yacc_ref[pl.ds(r0, group), :] += _dot(act, wd_ref[0])

        over_groups(expert)

        @pl.when(j == last_j)
        def _():
            lane = lax.broadcasted_iota(jnp.int32, (1, LANES), 1)
            rank_col = jnp.sum(jnp.where(lane == e, rank_ref[...], 0.0), axis=-1, keepdims=True)
            gate_col = jnp.sum(jnp.where(lane == e, gates_ref[...], 0.0), axis=-1, keepdims=True)

            def expand(r0):
                slot = (lax.broadcasted_iota(jnp.int32, (1, group), 1) + r0).astype(F32)
                onehot = jnp.where(rank_col == slot, 1.0, 0.0).astype(BF16)
                acc_ref[...] += gate_col * _dot(onehot, yacc_ref[pl.ds(r0, group), :].astype(BF16))

            over_groups(expand)

    @pl.when(jnp.logical_and(n_rows > 0, n_rows <= MOE_GROUP_SMALL))
    def _():
        run(MOE_GROUP_SMALL, 1)

    @pl.when(jnp.logical_and(n_rows > MOE_GROUP_SMALL, n_rows <= MOE_GROUP))
    def _():
        run(MOE_GROUP, 1)

    @pl.when(n_rows > MOE_GROUP)
    def _():
        run(MOE_GROUP, (n_rows + (MOE_GROUP - 1)) // MOE_GROUP)

    @pl.when(jnp.logical_and(e == last_e, j == last_j))
    def _():
        o_ref[...] = h_ref[...] + _rms(acc_ref[...], gout_ref[...])


def moe_block(h, g_in, w_router, wg, wu, wd, g_out, tm=MOE_TM, tf=MOE_TF):
    t, d = h.shape
    n_e, _, f = wg.shape
    n_tiles = t // tm
    wr = jnp.pad(w_router, ((0, 0), (0, LANES - n_e)))
    xn, gates, rank, rankt, counts = pl.pallas_call(
        _router_kernel,
        grid=(n_tiles,),
        in_specs=[
            pl.BlockSpec((tm, d), lambda i: (i, 0)),
            pl.BlockSpec((1, d), lambda i: (0, 0)),
            pl.BlockSpec((d, LANES), lambda i: (0, 0)),
        ],
        out_specs=[
            pl.BlockSpec((tm, d), lambda i: (i, 0)),
            pl.BlockSpec((tm, LANES), lambda i: (i, 0)),
            pl.BlockSpec((tm, LANES), lambda i: (i, 0)),
            pl.BlockSpec((1, LANES, tm), lambda i: (i, 0, 0)),
            pl.BlockSpec((1, 1, LANES), lambda i: (i, 0, 0)),
        ],
        out_shape=[
            jax.ShapeDtypeStruct((t, d), BF16),
            jax.ShapeDtypeStruct((t, LANES), F32),
            jax.ShapeDtypeStruct((t, LANES), F32),
            jax.ShapeDtypeStruct((n_tiles, LANES, tm), F32),
            jax.ShapeDtypeStruct((n_tiles, 1, LANES), F32),
        ],
        compiler_params=pltpu.CompilerParams(
            dimension_semantics=("parallel",), vmem_limit_bytes=VMEM_LIMIT),
        name="moe_router",
    )(h, g_in.reshape(1, d), wr)
    n_rows = counts[:, 0, :n_e].astype(jnp.int32)
    max_rows = -(-tm // MOE_GROUP) * MOE_GROUP
    grid_spec = pltpu.PrefetchScalarGridSpec(
        num_scalar_prefetch=1,
        grid=(n_tiles, n_e, f // tf),
        in_specs=[
            pl.BlockSpec((tm, d), lambda i, e, j, ng: (i, 0)),
            pl.BlockSpec((tm, LANES), lambda i, e, j, ng: (i, 0)),
            pl.BlockSpec((tm, LANES), lambda i, e, j, ng: (i, 0)),
            pl.BlockSpec((1, LANES, tm), lambda i, e, j, ng: (i, 0, 0)),
            pl.BlockSpec((1, d, tf), lambda i, e, j, ng: (e, 0, j)),
            pl.BlockSpec((1, d, tf), lambda i, e, j, ng: (e, 0, j)),
            pl.BlockSpec((1, tf, d), lambda i, e, j, ng: (e, j, 0)),
            pl.BlockSpec((tm, d), lambda i, e, j, ng: (i, 0)),
            pl.BlockSpec((1, d), lambda i, e, j, ng: (0, 0)),
        ],
        out_specs=pl.BlockSpec((tm, d), lambda i, e, j, ng: (i, 0)),
        scratch_shapes=[
            pltpu.VMEM((max_rows, d), BF16),
            pltpu.VMEM((max_rows, d), F32),
            pltpu.VMEM((tm, d), F32),
        ],
    )
    return pl.pallas_call(
        _moe_kernel,
        grid_spec=grid_spec,
        out_shape=jax.ShapeDtypeStruct((t, d), F32),
        compiler_params=pltpu.CompilerParams(
            dimension_semantics=("parallel", "arbitrary", "arbitrary"), vmem_limit_bytes=VMEM_LIMIT),
        name="moe_swiglu",
    )(n_rows.reshape(-1), xn, gates, rank, rankt, wg, wu, wd, h, g_out.reshape(1, d))


def kernel(x, norm_g, na_w_qkv, na_rpb, na_w_o, gqa_w_qkv, gqa_q_norm, gqa_k_norm, gqa_w_o,
           ffn_w_gate, ffn_w_up, ffn_w_down, moe_w_router, moe_w_gate, moe_w_up, moe_w_down):
    assert x.shape == (BATCH, SEQ, D_MODEL)
    h = x.reshape(TOKENS, D_MODEL)

    log2e = float(np.log2(np.e))
    qt0, k0, vt0 = na_qkv(h, norm_g[0, 0], na_w_qkv[0].astype(BF16), (HEAD_DIM ** -0.5) * log2e)
    a = neighborhood_attention(qt0, k0, vt0, _na_bias_tables(na_rpb[0] * log2e))
    h = proj_norm_res(a, na_w_o[0].astype(BF16), norm_g[0, 1], h)
    h = ffn_block(h, norm_g[0, 2], ffn_w_gate[0].astype(BF16), ffn_w_up[0].astype(BF16),
                  ffn_w_down[0].astype(BF16), norm_g[0, 3])

    nq = GQA_Q_HEADS * HEAD_DIM
    nkv = GQA_KV_HEADS * HEAD_DIM
    wq, wk, wv = gqa_w_qkv[0][:, :nq], gqa_w_qkv[0][:, nq:nq + nkv], gqa_w_qkv[0][:, nq + nkv:]
    qt, k, vt = gqa_qkv(h, norm_g[1, 0], wq, wk, wv, gqa_q_norm[0], gqa_k_norm[0])
    o = flash_attention(qt, k, vt)
    h = proj_norm_res(o, gqa_w_o[0].astype(BF16), norm_g[1, 1], h)
    h = moe_block(h, norm_g[1, 2], moe_w_router[0], moe_w_gate[0].astype(BF16), moe_w_up[0].astype(BF16),
                  moe_w_down[0].astype(BF16), norm_g[1, 3])
    return h.reshape(BATCH, SEQ, D_MODEL)
```

```python
import functools

import jax
import jax.numpy as jnp
import numpy as np
from jax import lax
from jax.experimental import pallas as pl
from jax.experimental.pallas import tpu as pltpu

F32 = jnp.float32
BF16 = jnp.bfloat16

D_MODEL = 1024
BATCH = 2
SEQ = 8192
TOKENS = BATCH * SEQ
GRID_W = 64
GRID_H = SEQ // GRID_W
NA_HEADS = 16
HEAD_DIM = 64
WIN_H = 8
WIN_W = 16
GQA_Q_HEADS = 16
GQA_KV_HEADS = 4
GQA_GROUPS = GQA_Q_HEADS // GQA_KV_HEADS
ROPE_THETA = 10000.0
D_FF = 3584
N_EXPERTS = 8
EPS = 1e-6
LANES = 128
NEG_BIG = -1e30

VMEM_LIMIT = 56 * 1024 * 1024

NA_QROWS = 4
NA_QTOK = NA_QROWS * GRID_W
NA_BAND_BLOCKS = 3
NA_KTOK = NA_BAND_BLOCKS * NA_QTOK
NA_ROW_BLOCKS = GRID_H // NA_QROWS
NA_HEADS_PER_STEP = 16

FA_TQ = 512
FA_TK = 512
MOE_TM = 1024
MOE_TF = 896
MOE_GROUP = 384
MOE_GROUP_SMALL = 256
MOE_FFN_ROWS = (288, 320, MOE_GROUP)

FA_UNROLL = 8
FA_INIT_KEYS = 16
FA_MAX_JUMP = 24.0
FA_VROWS = 80


def _rms(x, g):
    ms = jnp.mean(x * x, axis=-1, keepdims=True)
    return x * lax.rsqrt(ms + EPS) * g


def _dot(a, b):
    return jnp.dot(a, b, preferred_element_type=F32)


def _dot_nt(a, b):
    return lax.dot_general(a, b, (((1,), (1,)), ((), ())), preferred_element_type=F32)


def _na_qkv_kernel(x_ref, g_ref, w_ref, qt_ref, k_ref, vt_ref, *, q_scale, n_chunk):
    xn = _rms(x_ref[...], g_ref[...]).astype(BF16)
    tm = xn.shape[0]
    d = k_ref.shape[1]

    def store_transposed(dst_ref, y, col0):
        for t0 in range(0, tm, NA_QTOK):
            for c0 in range(0, n_chunk, LANES):
                dst_ref[t0 // NA_QTOK, col0 + c0:col0 + c0 + LANES, :] = (
                    y[t0:t0 + NA_QTOK, c0:c0 + LANES].T.astype(BF16))

    for n0 in range(0, d, n_chunk):
        store_transposed(qt_ref, _dot(xn, w_ref[:, n0:n0 + n_chunk]) * q_scale, n0)
        k_ref[:, n0:n0 + n_chunk] = _dot(xn, w_ref[:, d + n0:d + n0 + n_chunk]).astype(BF16)
        store_transposed(vt_ref, _dot(xn, w_ref[:, 2 * d + n0:2 * d + n0 + n_chunk]), n0)


def na_qkv(x, g, w, q_scale, tm=512, n_chunk=512):
    t, d = x.shape
    n = w.shape[1]
    tiles = tm // NA_QTOK
    transposed = jax.ShapeDtypeStruct((t // NA_QTOK, d, NA_QTOK), BF16)
    return pl.pallas_call(
        functools.partial(_na_qkv_kernel, q_scale=q_scale, n_chunk=n_chunk),
        grid=(t // tm,),
        in_specs=[
            pl.BlockSpec((tm, d), lambda i: (i, 0)),
            pl.BlockSpec((1, d), lambda i: (0, 0)),
            pl.BlockSpec((d, n), lambda i: (0, 0)),
        ],
        out_specs=[
            pl.BlockSpec((tiles, d, NA_QTOK), lambda i: (i, 0, 0)),
            pl.BlockSpec((tm, d), lambda i: (i, 0)),
            pl.BlockSpec((tiles, d, NA_QTOK), lambda i: (i, 0, 0)),
        ],
        out_shape=[transposed, jax.ShapeDtypeStruct((t, d), BF16), transposed],
        compiler_params=pltpu.CompilerParams(
            dimension_semantics=("parallel",), vmem_limit_bytes=VMEM_LIMIT),
        name="na_qkv",
    )(x, g.reshape(1, d), w)


def _proj_norm_res_kernel(a_ref, w_ref, g_ref, h_ref, o_ref):
    y = _dot(a_ref[...], w_ref[...])
    o_ref[...] = h_ref[...] + _rms(y, g_ref[...])


def proj_norm_res(a, w, g, h, tm=512):
    t, k = a.shape
    d = w.shape[1]
    return pl.pallas_call(
        _proj_norm_res_kernel,
        grid=(t // tm,),
        in_specs=[
            pl.BlockSpec((tm, k), lambda i: (i, 0)),
            pl.BlockSpec((k, d), lambda i: (0, 0)),
            pl.BlockSpec((1, d), lambda i: (0, 0)),
            pl.BlockSpec((tm, d), lambda i: (i, 0)),
        ],
        out_specs=pl.BlockSpec((tm, d), lambda i: (i, 0)),
        out_shape=jax.ShapeDtypeStruct((t, d), F32),
        compiler_params=pltpu.CompilerParams(
            dimension_semantics=("parallel",), vmem_limit_bytes=VMEM_LIMIT),
        name="proj_norm_res",
    )(a, w, g.reshape(1, d), h)


def _na_bias_expand_kernel(plane_ref, cols_ref, o_ref):
    case = pl.program_id(0)
    band_rows = NA_BAND_BLOCKS * NA_QROWS
    for i in range(NA_QROWS):
        for m in range(band_rows):
            u = plane_ref[(case * NA_QROWS + i) * band_rows + m]
            o_ref[0, 0, m * GRID_W:(m + 1) * GRID_W, i * GRID_W:(i + 1) * GRID_W] = cols_ref[0, u]


def _na_bias_tables(rpb):
    i = np.arange(NA_QROWS)[:, None]
    m = np.arange(NA_BAND_BLOCKS * NA_QROWS)[None, :]
    ridx, rvalid = [], []
    for rb in (0, 1, NA_ROW_BLOCKS - 1):
        band0 = NA_QROWS * min(max(rb - 1, 0), NA_ROW_BLOCKS - NA_BAND_BLOCKS)
        r = NA_QROWS * rb + i
        kr = band0 + m
        rs = np.clip(r - WIN_H // 2, 0, GRID_H - WIN_H)
        rvalid.append((kr >= rs) & (kr < rs + WIN_H))
        ridx.append(np.clip(kr - r + WIN_H - 1, 0, 2 * WIN_H - 2))
    n_dr = 2 * WIN_H - 1
    plane = np.where(np.stack(rvalid), np.stack(ridx), n_dr)
    c = np.arange(GRID_W)[:, None]
    kc = np.arange(GRID_W)[None, :]
    cs = np.clip(c - WIN_W // 2, 0, GRID_W - WIN_W)
    cvalid = (kc >= cs) & (kc < cs + WIN_W)
    valid_t = cvalid.T
    pick = ((kc.T - c.T + WIN_W - 1)[None] == np.arange(2 * WIN_W - 1)[:, None, None]) & valid_t[None]
    cols_t = jnp.einsum('hud,dkc->hukc', rpb, jnp.asarray(pick, F32), precision=lax.Precision.HIGHEST)
    cols_t = jnp.where(valid_t[None, None], cols_t, NEG_BIG)
    cols_t = jnp.concatenate([cols_t, jnp.full_like(cols_t[:, :1], NEG_BIG)], axis=1)
    n_cases = plane.shape[0]
    grid_spec = pltpu.PrefetchScalarGridSpec(
        num_scalar_prefetch=1,
        grid=(n_cases, NA_HEADS),
        in_specs=[pl.BlockSpec((1, n_dr + 1, GRID_W, GRID_W), lambda z, h, pr: (h, 0, 0, 0))],
        out_specs=pl.BlockSpec((1, 1, NA_KTOK, NA_QTOK), lambda z, h, pr: (z, h // 2, 0, h % 2)),
    )
    return pl.pallas_call(
        _na_bias_expand_kernel,
        grid_spec=grid_spec,
        out_shape=jax.ShapeDtypeStruct((n_cases, NA_HEADS // 2, NA_KTOK, 2 * NA_QTOK), F32),
        compiler_params=pltpu.CompilerParams(dimension_semantics=("parallel", "parallel")),
        name="na_bias_expand",
    )(jnp.asarray(plane.reshape(-1), jnp.int32), cols_t)


def _na_kernel(qt_ref, k0_ref, k1_ref, k2_ref, vt0_ref, vt1_ref, vt2_ref, bias_ref, o_ref):
    row = lax.broadcasted_iota(jnp.int32, (LANES, 1), 0)
    low = row < HEAD_DIM

    n_pairs = NA_HEADS_PER_STEP // 2

    def scores(pair):
        rows = slice(pair * LANES, (pair + 1) * LANES)
        qt = qt_ref[0, rows, :]
        k = jnp.concatenate([k0_ref[:, rows], k1_ref[:, rows], k2_ref[:, rows]], axis=0)
        zero = jnp.zeros_like(qt)
        qt2 = jnp.concatenate([jnp.where(low, qt, zero), jnp.where(low, zero, qt)], axis=1)
        return _dot(k, qt2)

    s = scores(0)
    for pair in range(n_pairs):
        s_next = scores(pair + 1) if pair + 1 < n_pairs else None
        rows = slice(pair * LANES, (pair + 1) * LANES)
        vt = jnp.concatenate([vt0_ref[0, rows, :], vt1_ref[0, rows, :], vt2_ref[0, rows, :]], axis=1)
        one = jnp.ones_like(vt)
        s = s + bias_ref[0, pair]
        p = jnp.exp2(s - jnp.max(s, axis=0, keepdims=True)).astype(BF16)
        pv_a = _dot(jnp.where(low, vt, one), p[:, :NA_QTOK])
        pv_b = _dot(jnp.where(low, one, vt), p[:, NA_QTOK:])
        both = jnp.concatenate([pv_a[:HEAD_DIM] / pv_a[HEAD_DIM:], pv_b[HEAD_DIM:] / pv_b[:HEAD_DIM]], axis=0)
        o_ref[:, rows] = both.T.astype(o_ref.dtype)
        s = s_next


def neighborhood_attention(qt, k, vt, bias_tab):
    hw = NA_HEADS_PER_STEP * HEAD_DIM
    nh = D_MODEL // hw
    blocks_per_seq = SEQ // NA_QTOK

    def band(rb):
        return jnp.clip(rb - 1, 0, NA_ROW_BLOCKS - NA_BAND_BLOCKS)

    def case(rb):
        return jnp.where(rb == 0, 0, jnp.where(rb == NA_ROW_BLOCKS - 1, 2, 1))

    qt_spec = pl.BlockSpec((1, hw, NA_QTOK), lambda h, b, rb: (b * blocks_per_seq + rb, h, 0))
    k_specs = [pl.BlockSpec((NA_QTOK, hw), lambda h, b, rb, j=j: (b * blocks_per_seq + band(rb) + j, h))
               for j in range(NA_BAND_BLOCKS)]
    vt_specs = [pl.BlockSpec((1, hw, NA_QTOK), lambda h, b, rb, j=j: (b * blocks_per_seq + band(rb) + j, h, 0))
                for j in range(NA_BAND_BLOCKS)]
    bias_spec = pl.BlockSpec((1, NA_HEADS_PER_STEP // 2, NA_KTOK, 2 * NA_QTOK),
                             lambda h, b, rb: (case(rb), h, 0, 0))
    return pl.pallas_call(
        _na_kernel,
        grid=(nh, BATCH, NA_ROW_BLOCKS),
        in_specs=[qt_spec] + k_specs + vt_specs + [bias_spec],
        out_specs=pl.BlockSpec((NA_QTOK, hw), lambda h, b, rb: (b * blocks_per_seq + rb, h)),
        out_shape=jax.ShapeDtypeStruct((TOKENS, D_MODEL), BF16),
        compiler_params=pltpu.CompilerParams(
            dimension_semantics=("parallel", "parallel", "parallel"), vmem_limit_bytes=VMEM_LIMIT),
        name="na_attention",
    )(qt, k, k, k, vt, vt, vt, bias_tab)


def _ffn_kernel(h_ref, gin_ref, wg_ref, wu_ref, wd_ref, gout_ref, o_ref, xn_ref, acc_ref):
    j = pl.program_id(1)

    @pl.when(j == 0)
    def _():
        xn_ref[...] = _rms(h_ref[...], gin_ref[...]).astype(BF16)
        acc_ref[...] = jnp.zeros_like(acc_ref)

    xn = xn_ref[...]
    gate = _dot(xn, wg_ref[...])
    up = _dot(xn, wu_ref[...])
    act = (gate * jax.nn.sigmoid(gate) * up).astype(BF16)
    acc_ref[...] += _dot(act, wd_ref[...])

    @pl.when(j == pl.num_programs(1) - 1)
    def _():
        o_ref[...] = h_ref[...] + _rms(acc_ref[...], gout_ref[...])


def ffn_block(h, g_in, wg, wu, wd, g_out, tm=1024, tf=512):
    t, d = h.shape
    f = wg.shape[1]
    return pl.pallas_call(
        _ffn_kernel,
        grid=(t // tm, f // tf),
        in_specs=[
            pl.BlockSpec((tm, d), lambda i, j: (i, 0)),
            pl.BlockSpec((1, d), lambda i, j: (0, 0)),
            pl.BlockSpec((d, tf), lambda i, j: (0, j)),
            pl.BlockSpec((d, tf), lambda i, j: (0, j)),
            pl.BlockSpec((tf, d), lambda i, j: (j, 0)),
            pl.BlockSpec((1, d), lambda i, j: (0, 0)),
        ],
        out_specs=pl.BlockSpec((tm, d), lambda i, j: (i, 0)),
        out_shape=jax.ShapeDtypeStruct((t, d), F32),
        scratch_shapes=[pltpu.VMEM((tm, d), BF16), pltpu.VMEM((tm, d), F32)],
        compiler_params=pltpu.CompilerParams(
            dimension_semantics=("parallel", "arbitrary"), vmem_limit_bytes=VMEM_LIMIT),
        name="ffn_swiglu",
    )(h, g_in.reshape(1, d), wg, wu, wd, g_out.reshape(1, d))


def _pad_heads(w, n_heads, width):
    d = w.shape[0]
    w = w.reshape(d, n_heads, HEAD_DIM)
    return jnp.pad(w, ((0, 0), (0, 0), (0, width - HEAD_DIM))).reshape(d, n_heads * width)


def _rope_tables():
    quarter = HEAD_DIM // 4
    freqs = ROPE_THETA ** (-jnp.arange(quarter, dtype=F32) / quarter)
    t = jnp.arange(SEQ)
    row_ang = (t // GRID_W).astype(F32)[:, None] * freqs[None, :]
    col_ang = (t % GRID_W).astype(F32)[:, None] * freqs[None, :]
    zeros = jnp.zeros((SEQ, LANES - HEAD_DIM), F32)
    cos = jnp.concatenate([jnp.cos(row_ang), jnp.cos(row_ang), jnp.cos(col_ang), jnp.cos(col_ang), zeros], axis=1)
    sin = jnp.concatenate([-jnp.sin(row_ang), jnp.sin(row_ang), -jnp.sin(col_ang), jnp.sin(col_ang), zeros], axis=1)
    return cos, sin


def _gqa_qkv_kernel(x_ref, g_ref, wq_ref, wk_ref, wv_ref, qg_ref, kg_ref, cos_ref, sin_ref, cost_ref, sint_ref,
                    qt_ref, k_ref, vt_ref):
    xn = _rms(x_ref[...], g_ref[...]).astype(BF16)
    tm = xn.shape[0]
    quarter = HEAD_DIM // 4

    cost = cost_ref[...]
    sint = sint_ref[...]
    qg = qg_ref[...]
    zero_rows = jnp.zeros((LANES - HEAD_DIM, tm), BF16)
    wide = 2 * LANES
    for pair in range(GQA_Q_HEADS // 2):
        if pair % 2 == 0:
            y_wide = _dot(xn, wq_ref[:, pair * LANES:pair * LANES + wide])
        yt = y_wide[:, (pair % 2) * LANES:(pair % 2 + 1) * LANES].T
        for half in range(2):
            y = yt[half * HEAD_DIM:(half + 1) * HEAD_DIM]
            ms = jnp.sum(y * y, axis=0, keepdims=True) * (1.0 / HEAD_DIM)
            y = y * lax.rsqrt(ms + EPS) * qg
            partner = jnp.concatenate(
                [y[quarter:2 * quarter], y[:quarter], y[3 * quarter:], y[2 * quarter:3 * quarter]], axis=0)
            base = (2 * pair + half) * LANES
            qt_ref[0, base:base + HEAD_DIM, :] = (y * cost + partner * sint).astype(BF16)
            qt_ref[0, base + HEAD_DIM:base + LANES, :] = zero_rows

    lane = lax.broadcasted_iota(jnp.int32, (1, LANES), 1)
    first = (lane % (2 * quarter)) < quarter
    cos = cos_ref[...]
    sin = sin_ref[...]
    for hh in range(GQA_KV_HEADS):
        if hh % 2 == 0:
            y_wide = _dot(xn, wk_ref[:, hh * LANES:hh * LANES + wide])
        y = y_wide[:, (hh % 2) * LANES:(hh % 2 + 1) * LANES]
        ms = jnp.sum(y * y, axis=-1, keepdims=True) * (1.0 / HEAD_DIM)
        y = y * lax.rsqrt(ms + EPS) * kg_ref[...]
        partner = jnp.where(first, pltpu.roll(y, LANES - quarter, 1), pltpu.roll(y, quarter, 1))
        k_ref[:, hh * LANES:(hh + 1) * LANES] = (y * cos + partner * sin).astype(BF16)

    fill = FA_VROWS - HEAD_DIM
    ones_then_zeros = jnp.where(lax.broadcasted_iota(jnp.int32, (fill, tm), 0) == 0, 1.0, 0.0).astype(BF16)
    assert GQA_KV_HEADS * HEAD_DIM == wide
    v_wide = _dot(xn, wv_ref[...])
    for pair in range(GQA_KV_HEADS // 2):
        vt = v_wide[:, pair * LANES:(pair + 1) * LANES].T
        for half in range(2):
            base = (2 * pair + half) * FA_VROWS
            vt_ref[0, base:base + HEAD_DIM, :] = vt[half * HEAD_DIM:(half + 1) * HEAD_DIM].astype(BF16)
            vt_ref[0, base + HEAD_DIM:base + FA_VROWS, :] = ones_then_zeros


def gqa_qkv(h, g, wq, wk, wv, q_gain, k_gain):
    t, d = h.shape
    tm = FA_TK
    seq_blocks = SEQ // tm
    wqb = wq.astype(BF16)
    wkp = _pad_heads(wk, GQA_KV_HEADS, LANES).astype(BF16)
    wvb = wv.astype(BF16)
    cos, sin = _rope_tables()
    q_scale = (HEAD_DIM ** -0.5) * float(np.log2(np.e))
    cost, sint = (cos[:, :HEAD_DIM] * q_scale).T, (sin[:, :HEAD_DIM] * q_scale).T
    qg = q_gain.reshape(HEAD_DIM, 1)
    kg = jnp.pad(k_gain, (0, LANES - HEAD_DIM)).reshape(1, LANES)
    const = lambda shape: pl.BlockSpec(shape, lambda i: (0,) * len(shape))
    return pl.pallas_call(
        _gqa_qkv_kernel,
        grid=(t // tm,),
        in_specs=[
            pl.BlockSpec((tm, d), lambda i: (i, 0)),
            const((1, d)),
            const(wqb.shape),
            const(wkp.shape),
            const(wvb.shape),
            const((HEAD_DIM, 1)),
            const((1, LANES)),
            pl.BlockSpec((tm, LANES), lambda i: (i % seq_blocks, 0)),
            pl.BlockSpec((tm, LANES), lambda i: (i % seq_blocks, 0)),
            pl.BlockSpec((HEAD_DIM, tm), lambda i: (0, i % seq_blocks)),
            pl.BlockSpec((HEAD_DIM, tm), lambda i: (0, i % seq_blocks)),
        ],
        out_specs=[
            pl.BlockSpec((1, GQA_Q_HEADS * LANES, tm), lambda i: (i, 0, 0)),
            pl.BlockSpec((tm, GQA_KV_HEADS * LANES), lambda i: (i, 0)),
            pl.BlockSpec((1, GQA_KV_HEADS * FA_VROWS, tm), lambda i: (i, 0, 0)),
        ],
        out_shape=[
            jax.ShapeDtypeStruct((t // tm, GQA_Q_HEADS * LANES, tm), BF16),
            jax.ShapeDtypeStruct((t, GQA_KV_HEADS * LANES), BF16),
            jax.ShapeDtypeStruct((t // tm, GQA_KV_HEADS * FA_VROWS, tm), BF16),
        ],
        compiler_params=pltpu.CompilerParams(
            dimension_semantics=("parallel",), vmem_limit_bytes=VMEM_LIMIT),
        name="gqa_qkv",
    )(h, g.reshape(1, d), wqb, wkp, wvb, qg, kg, cos, sin, cost, sint)


def _flash_kernel(qt_ref, k_ref, vt_ref, o_ref, m_ref, acc_ref, jump_ref):
    n_kv = SEQ // FA_TK

    def scores(j, g):
        kj = k_ref[pl.ds(pl.multiple_of(j * FA_TK, FA_TK), FA_TK), :]
        return _dot(kj, qt_ref[0, g * LANES:(g + 1) * LANES, :])

    def reset():
        m_ref[...] = jnp.full(m_ref.shape, -jnp.inf, F32)
        acc_ref[...] = jnp.zeros_like(acc_ref)

    def step_exact(j, g, s):
        m_prev = m_ref[g]
        m_new = jnp.maximum(m_prev, jnp.max(s, axis=0, keepdims=True))
        p = jnp.exp2(s - m_new)
        acc_ref[g] = jnp.exp2(m_prev - m_new) * acc_ref[g] + _dot(vt_ref[j], p.astype(BF16))
        m_ref[g] = m_new

    def step_lagged(j, g, s):
        m_prev = m_ref[g]
        p = jnp.exp2(s - m_prev)
        block_max = jnp.max(s, axis=0, keepdims=True)
        m_new = jnp.maximum(m_prev, block_max)
        acc_ref[g] = jnp.exp2(m_prev - m_new) * (acc_ref[g] + _dot(vt_ref[j], p.astype(BF16)))
        m_ref[g] = m_new
        jump_ref[g] = jnp.maximum(jump_ref[g], block_max - m_prev)

    def sweep(j, s, step):
        for g in range(GQA_GROUPS):
            if g + 1 < GQA_GROUPS:
                s_next = scores(j, g + 1)
            else:
                s_next = scores(jnp.minimum(j + 1, n_kv - 1), 0)
            step(j, g, s)
            s = s_next
        return s

    acc_ref[...] = jnp.zeros_like(acc_ref)
    jump_ref[...] = jnp.zeros_like(jump_ref)
    k_first = k_ref[0:FA_INIT_KEYS, :]
    for g in range(GQA_GROUPS):
        m_ref[g] = jnp.max(_dot(k_first, qt_ref[0, g * LANES:(g + 1) * LANES, :]), axis=0, keepdims=True)
    lax.fori_loop(0, n_kv, lambda j, s: sweep(j, s, step_lagged), scores(0, 0), unroll=FA_UNROLL)

    @pl.when(jnp.max(jump_ref[...]) > FA_MAX_JUMP)
    def _():
        reset()

        def body(j, carry):
            for g in range(GQA_GROUPS):
                step_exact(j, g, scores(j, g))
            return carry

        lax.fori_loop(0, n_kv, body, 0)

    for pair in range(GQA_GROUPS // 2):
        halves = []
        for g in (2 * pair, 2 * pair + 1):
            acc = acc_ref[g]
            halves.append(acc[:HEAD_DIM] / acc[HEAD_DIM:HEAD_DIM + 1])
        o_ref[:, pair * LANES:(pair + 1) * LANES] = jnp.concatenate(halves, axis=0).T.astype(o_ref.dtype)


def flash_attention(qt, k, vt):
    q_blocks = SEQ // FA_TQ
    q_per_tile = FA_TK // FA_TQ
    kv_blocks = SEQ // FA_TK
    gw = GQA_GROUPS * LANES
    return pl.pallas_call(
        _flash_kernel,
        grid=(BATCH, GQA_KV_HEADS, q_blocks),
        in_specs=[
            pl.BlockSpec((1, gw, FA_TQ), lambda b, kh, i: (b * kv_blocks + i // q_per_tile, kh, i % q_per_tile)),
            pl.BlockSpec((SEQ, LANES), lambda b, kh, i: (b, kh)),
            pl.BlockSpec((kv_blocks, FA_VROWS, FA_TK), lambda b, kh, i: (b, kh, 0)),
        ],
        out_specs=pl.BlockSpec((FA_TQ, GQA_GROUPS * HEAD_DIM), lambda b, kh, i: (b * q_blocks + i, kh)),
        out_shape=jax.ShapeDtypeStruct((TOKENS, D_MODEL), BF16),
        scratch_shapes=[
            pltpu.VMEM((GQA_GROUPS, 1, FA_TQ), F32),
            pltpu.VMEM((GQA_GROUPS, FA_VROWS, FA_TQ), F32),
            pltpu.VMEM((GQA_GROUPS, 1, FA_TQ), F32),
        ],
        compiler_params=pltpu.CompilerParams(
            dimension_semantics=("parallel", "parallel", "parallel"), vmem_limit_bytes=VMEM_LIMIT),
        name="gqa_flash",
    )(qt, k, vt)


def _router_kernel(h_ref, gin_ref, wr_ref, xn_ref, gates_ref, rank_ref, rankt_ref, count_ref):
    tm = h_ref.shape[0]
    lane = lax.broadcasted_iota(jnp.int32, (1, LANES), 1)
    x = _rms(h_ref[...], gin_ref[...])
    xn_ref[...] = x.astype(BF16)
    logits = jnp.dot(x, wr_ref[...], preferred_element_type=F32, precision=lax.Precision.HIGHEST)
    logits = jnp.where(lane < N_EXPERTS, logits, -jnp.inf)
    m1 = jnp.max(logits, axis=-1, keepdims=True)
    i1 = jnp.min(jnp.where(logits == m1, lane, LANES), axis=-1, keepdims=True)
    rest = jnp.where(lane == i1, -jnp.inf, logits)
    m2 = jnp.max(rest, axis=-1, keepdims=True)
    i2 = jnp.min(jnp.where(rest == m2, lane, LANES), axis=-1, keepdims=True)
    e2 = jnp.exp(m2 - m1)
    denom = 1.0 + e2
    gates_ref[...] = jnp.where(lane == i1, 1.0 / denom, 0.0) + jnp.where(lane == i2, e2 / denom, 0.0)
    chosen = jnp.logical_or(lane == i1, lane == i2)
    earlier = lax.broadcasted_iota(jnp.int32, (tm, tm), 1) < lax.broadcasted_iota(jnp.int32, (tm, tm), 0)
    prefix = _dot(jnp.where(earlier, 1.0, 0.0).astype(BF16), jnp.where(chosen, 1.0, 0.0).astype(BF16))
    rank = jnp.where(chosen, prefix, -1.0)
    rank_ref[...] = rank
    rankt_ref[0] = rank.T
    count_ref[0] = jnp.sum(jnp.where(chosen, 1.0, 0.0), axis=0, keepdims=True)


def _moe_kernel(cnt_ref, xn_ref, gates_ref, rank_ref, rankt_ref, wg_ref, wu_ref, wd_ref, h_ref, gout_ref, o_ref,
                xs_ref, yacc_ref, acc_ref):
    i = pl.program_id(0)
    e = pl.program_id(1)
    j = pl.program_id(2)
    last_e = pl.num_programs(1) - 1
    last_j = pl.num_programs(2) - 1
    n_rows = cnt_ref[i * N_EXPERTS + e]

    @pl.when(jnp.logical_and(e == 0, j == 0))
    def _():
        acc_ref[...] = jnp.zeros_like(acc_ref)

    def run(group, n_groups, ffn_rows=None):
        ffn_rows = ffn_rows or group

        def over_groups(fn):
            if isinstance(n_groups, int):
                for c in range(n_groups):
                    fn(c * group)
            else:
                def body(c, carry):
                    fn(pl.multiple_of(c * group, LANES))
                    return carry

                lax.fori_loop(0, n_groups, body, 0)

        @pl.when(j == 0)
        def _():
            rank_row = rankt_ref[0, pl.ds(e, 1), :]

            def compact(r0):
                slot = (lax.broadcasted_iota(jnp.int32, (group, 1), 0) + r0).astype(F32)
                onehot = jnp.where(rank_row == slot, 1.0, 0.0).astype(BF16)
                xs_ref[pl.ds(r0, group), :] = _dot(onehot, xn_ref[...]).astype(BF16)
                yacc_ref[pl.ds(r0, group), :] = jnp.zeros((group, yacc_ref.shape[1]), F32)

            over_groups(compact)

        def expert(r0):
            xs = xs_ref[pl.ds(r0, ffn_rows), :]
            gate = _dot(xs, wg_ref[0])
            up = _dot(xs, wu_ref[0])
            act = (gate * jax.nn.sigmoid(gate) * up).astype(BF16)
            yacc_ref[pl.ds(r0, ffn_rows), :] += _dot(act, wd_ref[0])

        over_groups(expert)

        @pl.when(j == last_j)
        def _():
            lane = lax.broadcasted_iota(jnp.int32, (1, LANES), 1)
            rank_col = jnp.sum(jnp.where(lane == e, rank_ref[...], 0.0), axis=-1, keepdims=True)
            gate_col = jnp.sum(jnp.where(lane == e, gates_ref[...], 0.0), axis=-1, keepdims=True)

            def expand(r0):
                slot = (lax.broadcasted_iota(jnp.int32, (1, group), 1) + r0).astype(F32)
                onehot = jnp.where(rank_col == slot, 1.0, 0.0).astype(BF16)
                acc_ref[...] += gate_col * _dot(onehot, yacc_ref[pl.ds(r0, group), :].astype(BF16))

            over_groups(expand)

    @pl.when(jnp.logical_and(n_rows > 0, n_rows <= MOE_GROUP_SMALL))
    def _():
        run(MOE_GROUP_SMALL, 1)

    lower = MOE_GROUP_SMALL
    for upper in MOE_FFN_ROWS:
        @pl.when(jnp.logical_and(n_rows > lower, n_rows <= upper))
        def _(upper=upper):
            run(MOE_GROUP, 1, upper)
        lower = upper

    @pl.when(n_rows > MOE_GROUP)
    def _():
        run(MOE_GROUP, (n_rows + (MOE_GROUP - 1)) // MOE_GROUP)

    @pl.when(jnp.logical_and(e == last_e, j == last_j))
    def _():
        o_ref[...] = h_ref[...] + _rms(acc_ref[...], gout_ref[...])


def moe_block(h, g_in, w_router, wg, wu, wd, g_out, tm=MOE_TM, tf=MOE_TF):
    t, d = h.shape
    n_e, _, f = wg.shape
    n_tiles = t // tm
    wr = jnp.pad(w_router, ((0, 0), (0, LANES - n_e)))
    xn, gates, rank, rankt, counts = pl.pallas_call(
        _router_kernel,
        grid=(n_tiles,),
        in_specs=[
            pl.BlockSpec((tm, d), lambda i: (i, 0)),
            pl.BlockSpec((1, d), lambda i: (0, 0)),
            pl.BlockSpec((d, LANES), lambda i: (0, 0)),
        ],
        out_specs=[
            pl.BlockSpec((tm, d), lambda i: (i, 0)),
            pl.BlockSpec((tm, LANES), lambda i: (i, 0)),
            pl.BlockSpec((tm, LANES), lambda i: (i, 0)),
            pl.BlockSpec((1, LANES, tm), lambda i: (i, 0, 0)),
            pl.BlockSpec((1, 1, LANES), lambda i: (i, 0, 0)),
        ],
        out_shape=[
            jax.ShapeDtypeStruct((t, d), BF16),
            jax.ShapeDtypeStruct((t, LANES), F32),
            jax.ShapeDtypeStruct((t, LANES), F32),
            jax.ShapeDtypeStruct((n_tiles, LANES, tm), F32),
            jax.ShapeDtypeStruct((n_tiles, 1, LANES), F32),
        ],
        compiler_params=pltpu.CompilerParams(
            dimension_semantics=("parallel",), vmem_limit_bytes=VMEM_LIMIT),
        name="moe_router",
    )(h, g_in.reshape(1, d), wr)
    n_rows = counts[:, 0, :n_e].astype(jnp.int32)
    max_rows = -(-tm // MOE_GROUP) * MOE_GROUP
    grid_spec = pltpu.PrefetchScalarGridSpec(
        num_scalar_prefetch=1,
        grid=(n_tiles, n_e, f // tf),
        in_specs=[
            pl.BlockSpec((tm, d), lambda i, e, j, ng: (i, 0)),
            pl.BlockSpec((tm, LANES), lambda i, e, j, ng: (i, 0)),
            pl.BlockSpec((tm, LANES), lambda i, e, j, ng: (i, 0)),
            pl.BlockSpec((1, LANES, tm), lambda i, e, j, ng: (i, 0, 0)),
            pl.BlockSpec((1, d, tf), lambda i, e, j, ng: (e, 0, j)),
            pl.BlockSpec((1, d, tf), lambda i, e, j, ng: (e, 0, j)),
            pl.BlockSpec((1, tf, d), lambda i, e, j, ng: (e, j, 0)),
            pl.BlockSpec((tm, d), lambda i, e, j, ng: (i, 0)),
            pl.BlockSpec((1, d), lambda i, e, j, ng: (0, 0)),
        ],
        out_specs=pl.BlockSpec((tm, d), lambda i, e, j, ng: (i, 0)),
        scratch_shapes=[
            pltpu.VMEM((max_rows, d), BF16),
            pltpu.VMEM((max_rows, d), F32),
            pltpu.VMEM((tm, d), F32),
        ],
    )
    return pl.pallas_call(
        _moe_kernel,
        grid_spec=grid_spec,
        out_shape=jax.ShapeDtypeStruct((t, d), F32),
        compiler_params=pltpu.CompilerParams(
            dimension_semantics=("parallel", "arbitrary", "arbitrary"), vmem_limit_bytes=VMEM_LIMIT),
        name="moe_swiglu",
    )(n_rows.reshape(-1), xn, gates, rank, rankt, wg, wu, wd, h, g_out.reshape(1, d))


def kernel(x, norm_g, na_w_qkv, na_rpb, na_w_o, gqa_w_qkv, gqa_q_norm, gqa_k_norm, gqa_w_o,
           ffn_w_gate, ffn_w_up, ffn_w_down, moe_w_router, moe_w_gate, moe_w_up, moe_w_down):
    assert x.shape == (BATCH, SEQ, D_MODEL)
    h = x.reshape(TOKENS, D_MODEL)

    log2e = float(np.log2(np.e))
    qt0, k0, vt0 = na_qkv(h, norm_g[0, 0], na_w_qkv[0].astype(BF16), (HEAD_DIM ** -0.5) * log2e)
    a = neighborhood_attention(qt0, k0, vt0, _na_bias_tables(na_rpb[0] * log2e))
    h = proj_norm_res(a, na_w_o[0].astype(BF16), norm_g[0, 1], h)
    h = ffn_block(h, norm_g[0, 2], ffn_w_gate[0].astype(BF16), ffn_w_up[0].astype(BF16),
                  ffn_w_down[0].astype(BF16), norm_g[0, 3])

    nq = GQA_Q_HEADS * HEAD_DIM
    nkv = GQA_KV_HEADS * HEAD_DIM
    wq, wk, wv = gqa_w_qkv[0][:, :nq], gqa_w_qkv[0][:, nq:nq + nkv], gqa_w_qkv[0][:, nq + nkv:]
    qt, k, vt = gqa_qkv(h, norm_g[1, 0], wq, wk, wv, gqa_q_norm[0], gqa_k_norm[0])
    o = flash_attention(qt, k, vt)
    h = proj_norm_res(o, gqa_w_o[0].astype(BF16), norm_g[1, 1], h)
    h = moe_block(h, norm_g[1, 2], moe_w_router[0], moe_w_gate[0].astype(BF16), moe_w_up[0].astype(BF16),
                  moe_w_down[0].astype(BF16), norm_g[1, 3])
    return h.reshape(BATCH, SEQ, D_MODEL)
```

```python
import functools

import jax
import jax.numpy as jnp
import numpy as np
from jax import lax
from jax.experimental import pallas as pl
from jax.experimental.pallas import tpu as pltpu

F32 = jnp.float32
BF16 = jnp.bfloat16

D_MODEL = 1024
BATCH = 2
SEQ = 8192
TOKENS = BATCH * SEQ
GRID_W = 64
GRID_H = SEQ // GRID_W
NA_HEADS = 16
HEAD_DIM = 64
WIN_H = 8
WIN_W = 16
GQA_Q_HEADS = 16
GQA_KV_HEADS = 4
GQA_GROUPS = GQA_Q_HEADS // GQA_KV_HEADS
ROPE_THETA = 10000.0
D_FF = 3584
N_EXPERTS = 8
EPS = 1e-6
LANES = 128
NEG_BIG = -1e30

VMEM_LIMIT = 56 * 1024 * 1024

NA_QROWS = 4
NA_QTOK = NA_QROWS * GRID_W
NA_BAND_BLOCKS = 3
NA_KTOK = NA_BAND_BLOCKS * NA_QTOK
NA_ROW_BLOCKS = GRID_H // NA_QROWS
NA_HEADS_PER_STEP = 16

FA_TQ = 512
FA_TK = 512
MOE_TM = 1024
MOE_TF = 896
MOE_GROUP = 384
MOE_GROUP_SMALL = 256
MOE_FFN_ROWS = (288, 320, MOE_GROUP)

FA_UNROLL = 16
FA_INIT_KEYS = 16
FA_MAX_JUMP = 24.0
FA_VROWS = 80


def _rms(x, g):
    ms = jnp.mean(x * x, axis=-1, keepdims=True)
    return x * lax.rsqrt(ms + EPS) * g


def _dot(a, b):
    return jnp.dot(a, b, preferred_element_type=F32)


def _dot_nt(a, b):
    return lax.dot_general(a, b, (((1,), (1,)), ((), ())), preferred_element_type=F32)


def _na_qkv_kernel(x_ref, g_ref, w_ref, qt_ref, k_ref, vt_ref, *, q_scale, n_chunk):
    xn = _rms(x_ref[...], g_ref[...]).astype(BF16)
    tm = xn.shape[0]
    d = k_ref.shape[1]

    def store_transposed(dst_ref, y, col0):
        for t0 in range(0, tm, NA_QTOK):
            for c0 in range(0, n_chunk, LANES):
                dst_ref[t0 // NA_QTOK, col0 + c0:col0 + c0 + LANES, :] = (
                    y[t0:t0 + NA_QTOK, c0:c0 + LANES].T.astype(BF16))

    for n0 in range(0, d, n_chunk):
        store_transposed(qt_ref, _dot(xn, w_ref[:, n0:n0 + n_chunk]) * q_scale, n0)
        k_ref[:, n0:n0 + n_chunk] = _dot(xn, w_ref[:, d + n0:d + n0 + n_chunk]).astype(BF16)
        store_transposed(vt_ref, _dot(xn, w_ref[:, 2 * d + n0:2 * d + n0 + n_chunk]), n0)


def na_qkv(x, g, w, q_scale, tm=512, n_chunk=512):
    t, d = x.shape
    n = w.shape[1]
    tiles = tm // NA_QTOK
    transposed = jax.ShapeDtypeStruct((t // NA_QTOK, d, NA_QTOK), BF16)
    return pl.pallas_call(
        functools.partial(_na_qkv_kernel, q_scale=q_scale, n_chunk=n_chunk),
        grid=(t // tm,),
        in_specs=[
            pl.BlockSpec((tm, d), lambda i: (i, 0)),
            pl.BlockSpec((1, d), lambda i: (0, 0)),
            pl.BlockSpec((d, n), lambda i: (0, 0)),
        ],
        out_specs=[
            pl.BlockSpec((tiles, d, NA_QTOK), lambda i: (i, 0, 0)),
            pl.BlockSpec((tm, d), lambda i: (i, 0)),
            pl.BlockSpec((tiles, d, NA_QTOK), lambda i: (i, 0, 0)),
        ],
        out_shape=[transposed, jax.ShapeDtypeStruct((t, d), BF16), transposed],
        compiler_params=pltpu.CompilerParams(
            dimension_semantics=("parallel",), vmem_limit_bytes=VMEM_LIMIT),
        name="na_qkv",
    )(x, g.reshape(1, d), w)


def _proj_norm_res_kernel(a_ref, w_ref, g_ref, h_ref, o_ref):
    y = _dot(a_ref[...], w_ref[...])
    o_ref[...] = h_ref[...] + _rms(y, g_ref[...])


def proj_norm_res(a, w, g, h, tm=512):
    t, k = a.shape
    d = w.shape[1]
    return pl.pallas_call(
        _proj_norm_res_kernel,
        grid=(t // tm,),
        in_specs=[
            pl.BlockSpec((tm, k), lambda i: (i, 0)),
            pl.BlockSpec((k, d), lambda i: (0, 0)),
            pl.BlockSpec((1, d), lambda i: (0, 0)),
            pl.BlockSpec((tm, d), lambda i: (i, 0)),
        ],
        out_specs=pl.BlockSpec((tm, d), lambda i: (i, 0)),
        out_shape=jax.ShapeDtypeStruct((t, d), F32),
        compiler_params=pltpu.CompilerParams(
            dimension_semantics=("parallel",), vmem_limit_bytes=VMEM_LIMIT),
        name="proj_norm_res",
    )(a, w, g.reshape(1, d), h)


def _na_bias_expand_kernel(plane_ref, cols_ref, o_ref):
    case = pl.program_id(0)
    band_rows = NA_BAND_BLOCKS * NA_QROWS
    for i in range(NA_QROWS):
        for m in range(band_rows):
            u = plane_ref[(case * NA_QROWS + i) * band_rows + m]
            o_ref[0, 0, m * GRID_W:(m + 1) * GRID_W, i * GRID_W:(i + 1) * GRID_W] = cols_ref[0, u]


def _na_bias_tables(rpb):
    i = np.arange(NA_QROWS)[:, None]
    m = np.arange(NA_BAND_BLOCKS * NA_QROWS)[None, :]
    ridx, rvalid = [], []
    for rb in (0, 1, NA_ROW_BLOCKS - 1):
        band0 = NA_QROWS * min(max(rb - 1, 0), NA_ROW_BLOCKS - NA_BAND_BLOCKS)
        r = NA_QROWS * rb + i
        kr = band0 + m
        rs = np.clip(r - WIN_H // 2, 0, GRID_H - WIN_H)
        rvalid.append((kr >= rs) & (kr < rs + WIN_H))
        ridx.append(np.clip(kr - r + WIN_H - 1, 0, 2 * WIN_H - 2))
    n_dr = 2 * WIN_H - 1
    plane = np.where(np.stack(rvalid), np.stack(ridx), n_dr)
    c = np.arange(GRID_W)[:, None]
    kc = np.arange(GRID_W)[None, :]
    cs = np.clip(c - WIN_W // 2, 0, GRID_W - WIN_W)
    cvalid = (kc >= cs) & (kc < cs + WIN_W)
    valid_t = cvalid.T
    pick = ((kc.T - c.T + WIN_W - 1)[None] == np.arange(2 * WIN_W - 1)[:, None, None]) & valid_t[None]
    cols_t = jnp.einsum('hud,dkc->hukc', rpb, jnp.asarray(pick, F32), precision=lax.Precision.HIGHEST)
    cols_t = jnp.where(valid_t[None, None], cols_t, NEG_BIG)
    cols_t = jnp.concatenate([cols_t, jnp.full_like(cols_t[:, :1], NEG_BIG)], axis=1)
    n_cases = plane.shape[0]
    grid_spec = pltpu.PrefetchScalarGridSpec(
        num_scalar_prefetch=1,
        grid=(n_cases, NA_HEADS),
        in_specs=[pl.BlockSpec((1, n_dr + 1, GRID_W, GRID_W), lambda z, h, pr: (h, 0, 0, 0))],
        out_specs=pl.BlockSpec((1, 1, NA_KTOK, NA_QTOK), lambda z, h, pr: (z, h // 2, 0, h % 2)),
    )
    return pl.pallas_call(
        _na_bias_expand_kernel,
        grid_spec=grid_spec,
        out_shape=jax.ShapeDtypeStruct((n_cases, NA_HEADS // 2, NA_KTOK, 2 * NA_QTOK), F32),
        compiler_params=pltpu.CompilerParams(dimension_semantics=("parallel", "parallel")),
        name="na_bias_expand",
    )(jnp.asarray(plane.reshape(-1), jnp.int32), cols_t)


def _na_kernel(qt_ref, k0_ref, k1_ref, k2_ref, vt0_ref, vt1_ref, vt2_ref, bias_ref, o_ref):
    row = lax.broadcasted_iota(jnp.int32, (LANES, 1), 0)
    low = row < HEAD_DIM

    n_pairs = NA_HEADS_PER_STEP // 2

    def scores(pair):
        rows = slice(pair * LANES, (pair + 1) * LANES)
        qt = qt_ref[0, rows, :]
        k = jnp.concatenate([k0_ref[:, rows], k1_ref[:, rows], k2_ref[:, rows]], axis=0)
        zero = jnp.zeros_like(qt)
        qt2 = jnp.concatenate([jnp.where(low, qt, zero), jnp.where(low, zero, qt)], axis=1)
        return _dot(k, qt2)

    s = scores(0)
    for pair in range(n_pairs):
        s_next = scores(pair + 1) if pair + 1 < n_pairs else None
        rows = slice(pair * LANES, (pair + 1) * LANES)
        vt = jnp.concatenate([vt0_ref[0, rows, :], vt1_ref[0, rows, :], vt2_ref[0, rows, :]], axis=1)
        one = jnp.ones_like(vt)
        s = s + bias_ref[0, pair]
        p = jnp.exp2(s - jnp.max(s, axis=0, keepdims=True)).astype(BF16)
        pv_a = _dot(jnp.where(low, vt, one), p[:, :NA_QTOK])
        pv_b = _dot(jnp.where(low, one, vt), p[:, NA_QTOK:])
        both = jnp.concatenate([pv_a[:HEAD_DIM] / pv_a[HEAD_DIM:], pv_b[HEAD_DIM:] / pv_b[:HEAD_DIM]], axis=0)
        o_ref[:, rows] = both.T.astype(o_ref.dtype)
        s = s_next


def neighborhood_attention(qt, k, vt, bias_tab):
    hw = NA_HEADS_PER_STEP * HEAD_DIM
    nh = D_MODEL // hw
    blocks_per_seq = SEQ // NA_QTOK

    def band(rb):
        return jnp.clip(rb - 1, 0, NA_ROW_BLOCKS - NA_BAND_BLOCKS)

    def case(rb):
        return jnp.where(rb == 0, 0, jnp.where(rb == NA_ROW_BLOCKS - 1, 2, 1))

    qt_spec = pl.BlockSpec((1, hw, NA_QTOK), lambda h, b, rb: (b * blocks_per_seq + rb, h, 0))
    k_specs = [pl.BlockSpec((NA_QTOK, hw), lambda h, b, rb, j=j: (b * blocks_per_seq + band(rb) + j, h))
               for j in range(NA_BAND_BLOCKS)]
    vt_specs = [pl.BlockSpec((1, hw, NA_QTOK), lambda h, b, rb, j=j: (b * blocks_per_seq + band(rb) + j, h, 0))
                for j in range(NA_BAND_BLOCKS)]
    bias_spec = pl.BlockSpec((1, NA_HEADS_PER_STEP // 2, NA_KTOK, 2 * NA_QTOK),
                             lambda h, b, rb: (case(rb), h, 0, 0))
    return pl.pallas_call(
        _na_kernel,
        grid=(nh, BATCH, NA_ROW_BLOCKS),
        in_specs=[qt_spec] + k_specs + vt_specs + [bias_spec],
        out_specs=pl.BlockSpec((NA_QTOK, hw), lambda h, b, rb: (b * blocks_per_seq + rb, h)),
        out_shape=jax.ShapeDtypeStruct((TOKENS, D_MODEL), BF16),
        compiler_params=pltpu.CompilerParams(
            dimension_semantics=("parallel", "parallel", "parallel"), vmem_limit_bytes=VMEM_LIMIT),
        name="na_attention",
    )(qt, k, k, k, vt, vt, vt, bias_tab)


def _ffn_kernel(h_ref, gin_ref, wg_ref, wu_ref, wd_ref, gout_ref, o_ref, xn_ref, acc_ref):
    j = pl.program_id(1)

    @pl.when(j == 0)
    def _():
        xn_ref[...] = _rms(h_ref[...], gin_ref[...]).astype(BF16)
        acc_ref[...] = jnp.zeros_like(acc_ref)

    xn = xn_ref[...]
    gate = _dot(xn, wg_ref[...])
    up = _dot(xn, wu_ref[...])
    act = (gate * jax.nn.sigmoid(gate) * up).astype(BF16)
    acc_ref[...] += _dot(act, wd_ref[...])

    @pl.when(j == pl.num_programs(1) - 1)
    def _():
        o_ref[...] = h_ref[...] + _rms(acc_ref[...], gout_ref[...])


def ffn_block(h, g_in, wg, wu, wd, g_out, tm=1024, tf=512):
    t, d = h.shape
    f = wg.shape[1]
    return pl.pallas_call(
        _ffn_kernel,
        grid=(t // tm, f // tf),
        in_specs=[
            pl.BlockSpec((tm, d), lambda i, j: (i, 0)),
            pl.BlockSpec((1, d), lambda i, j: (0, 0)),
            pl.BlockSpec((d, tf), lambda i, j: (0, j)),
            pl.BlockSpec((d, tf), lambda i, j: (0, j)),
            pl.BlockSpec((tf, d), lambda i, j: (j, 0)),
            pl.BlockSpec((1, d), lambda i, j: (0, 0)),
        ],
        out_specs=pl.BlockSpec((tm, d), lambda i, j: (i, 0)),
        out_shape=jax.ShapeDtypeStruct((t, d), F32),
        scratch_shapes=[pltpu.VMEM((tm, d), BF16), pltpu.VMEM((tm, d), F32)],
        compiler_params=pltpu.CompilerParams(
            dimension_semantics=("parallel", "arbitrary"), vmem_limit_bytes=VMEM_LIMIT),
        name="ffn_swiglu",
    )(h, g_in.reshape(1, d), wg, wu, wd, g_out.reshape(1, d))


def _pad_heads(w, n_heads, width):
    d = w.shape[0]
    w = w.reshape(d, n_heads, HEAD_DIM)
    return jnp.pad(w, ((0, 0), (0, 0), (0, width - HEAD_DIM))).reshape(d, n_heads * width)


def _rope_tables():
    f32 = np.float32
    quarter = HEAD_DIM // 4
    freqs = (f32(ROPE_THETA) ** (-np.arange(quarter, dtype=f32) / f32(quarter))).astype(f32)
    t = np.arange(SEQ)
    row_ang = (t // GRID_W).astype(f32)[:, None] * freqs[None, :]
    col_ang = (t % GRID_W).astype(f32)[:, None] * freqs[None, :]
    zeros = np.zeros((SEQ, LANES - HEAD_DIM), f32)
    cos = np.concatenate([np.cos(row_ang), np.cos(row_ang), np.cos(col_ang), np.cos(col_ang), zeros], axis=1)
    sin = np.concatenate([-np.sin(row_ang), np.sin(row_ang), -np.sin(col_ang), np.sin(col_ang), zeros], axis=1)
    return cos.astype(f32), sin.astype(f32)


def _gqa_qkv_kernel(x_ref, g_ref, wq_ref, wk_ref, wv_ref, qg_ref, kg_ref, cos_ref, sin_ref, cost_ref, sint_ref,
                    qt_ref, k_ref, vt_ref):
    xn = _rms(x_ref[...], g_ref[...]).astype(BF16)
    tm = xn.shape[0]
    quarter = HEAD_DIM // 4

    cost = cost_ref[...]
    sint = sint_ref[...]
    qg = qg_ref[...]
    zero_rows = jnp.zeros((LANES - HEAD_DIM, tm), BF16)
    wide = 2 * LANES
    for pair in range(GQA_Q_HEADS // 2):
        if pair % 2 == 0:
            y_wide = _dot(xn, wq_ref[:, pair * LANES:pair * LANES + wide])
        yt = y_wide[:, (pair % 2) * LANES:(pair % 2 + 1) * LANES].T
        for half in range(2):
            y = yt[half * HEAD_DIM:(half + 1) * HEAD_DIM]
            ms = jnp.sum(y * y, axis=0, keepdims=True) * (1.0 / HEAD_DIM)
            y = y * lax.rsqrt(ms + EPS) * qg
            partner = jnp.concatenate(
                [y[quarter:2 * quarter], y[:quarter], y[3 * quarter:], y[2 * quarter:3 * quarter]], axis=0)
            base = (2 * pair + half) * LANES
            qt_ref[0, base:base + HEAD_DIM, :] = (y * cost + partner * sint).astype(BF16)
            qt_ref[0, base + HEAD_DIM:base + LANES, :] = zero_rows

    lane = lax.broadcasted_iota(jnp.int32, (1, LANES), 1)
    first = (lane % (2 * quarter)) < quarter
    cos = cos_ref[...]
    sin = sin_ref[...]
    for hh in range(GQA_KV_HEADS):
        if hh % 2 == 0:
            y_wide = _dot(xn, wk_ref[:, hh * LANES:hh * LANES + wide])
        y = y_wide[:, (hh % 2) * LANES:(hh % 2 + 1) * LANES]
        ms = jnp.sum(y * y, axis=-1, keepdims=True) * (1.0 / HEAD_DIM)
        y = y * lax.rsqrt(ms + EPS) * kg_ref[...]
        partner = jnp.where(first, pltpu.roll(y, LANES - quarter, 1), pltpu.roll(y, quarter, 1))
        k_ref[:, hh * LANES:(hh + 1) * LANES] = (y * cos + partner * sin).astype(BF16)

    fill = FA_VROWS - HEAD_DIM
    ones_then_zeros = jnp.where(lax.broadcasted_iota(jnp.int32, (fill, tm), 0) == 0, 1.0, 0.0).astype(BF16)
    assert GQA_KV_HEADS * HEAD_DIM == wide
    v_wide = _dot(xn, wv_ref[...])
    for pair in range(GQA_KV_HEADS // 2):
        vt = v_wide[:, pair * LANES:(pair + 1) * LANES].T
        for half in range(2):
            base = (2 * pair + half) * FA_VROWS
            vt_ref[0, base:base + HEAD_DIM, :] = vt[half * HEAD_DIM:(half + 1) * HEAD_DIM].astype(BF16)
            vt_ref[0, base + HEAD_DIM:base + FA_VROWS, :] = ones_then_zeros


def gqa_qkv(h, g, wq, wk, wv, q_gain, k_gain):
    t, d = h.shape
    tm = FA_TK
    seq_blocks = SEQ // tm
    wqb = wq.astype(BF16)
    wkp = _pad_heads(wk, GQA_KV_HEADS, LANES).astype(BF16)
    wvb = wv.astype(BF16)
    cos, sin = _rope_tables()
    q_scale = (HEAD_DIM ** -0.5) * float(np.log2(np.e))
    cost = np.ascontiguousarray((cos[:, :HEAD_DIM] * np.float32(q_scale)).T)
    sint = np.ascontiguousarray((sin[:, :HEAD_DIM] * np.float32(q_scale)).T)
    qg = q_gain.reshape(HEAD_DIM, 1)
    kg = jnp.pad(k_gain, (0, LANES - HEAD_DIM)).reshape(1, LANES)
    const = lambda shape: pl.BlockSpec(shape, lambda i: (0,) * len(shape))
    return pl.pallas_call(
        _gqa_qkv_kernel,
        grid=(t // tm,),
        in_specs=[
            pl.BlockSpec((tm, d), lambda i: (i, 0)),
            const((1, d)),
            const(wqb.shape),
            const(wkp.shape),
            const(wvb.shape),
            const((HEAD_DIM, 1)),
            const((1, LANES)),
            pl.BlockSpec((tm, LANES), lambda i: (i % seq_blocks, 0)),
            pl.BlockSpec((tm, LANES), lambda i: (i % seq_blocks, 0)),
            pl.BlockSpec((HEAD_DIM, tm), lambda i: (0, i % seq_blocks)),
            pl.BlockSpec((HEAD_DIM, tm), lambda i: (0, i % seq_blocks)),
        ],
        out_specs=[
            pl.BlockSpec((1, GQA_Q_HEADS * LANES, tm), lambda i: (i, 0, 0)),
            pl.BlockSpec((tm, GQA_KV_HEADS * LANES), lambda i: (i, 0)),
            pl.BlockSpec((1, GQA_KV_HEADS * FA_VROWS, tm), lambda i: (i, 0, 0)),
        ],
        out_shape=[
            jax.ShapeDtypeStruct((t // tm, GQA_Q_HEADS * LANES, tm), BF16),
            jax.ShapeDtypeStruct((t, GQA_KV_HEADS * LANES), BF16),
            jax.ShapeDtypeStruct((t // tm, GQA_KV_HEADS * FA_VROWS, tm), BF16),
        ],
        compiler_params=pltpu.CompilerParams(
            dimension_semantics=("parallel",), vmem_limit_bytes=VMEM_LIMIT),
        name="gqa_qkv",
    )(h, g.reshape(1, d), wqb, wkp, wvb, qg, kg, cos, sin, cost, sint)


def _flash_kernel(qt_ref, k_ref, vt_ref, o_ref, m_ref, acc_ref, jump_ref):
    n_kv = SEQ // FA_TK

    def scores(j, g):
        kj = k_ref[pl.ds(pl.multiple_of(j * FA_TK, FA_TK), FA_TK), :]
        return _dot(kj, qt_ref[0, g * LANES:(g + 1) * LANES, :])

    def reset():
        m_ref[...] = jnp.full(m_ref.shape, -jnp.inf, F32)
        acc_ref[...] = jnp.zeros_like(acc_ref)

    def step_exact(j, g, s):
        m_prev = m_ref[g]
        m_new = jnp.maximum(m_prev, jnp.max(s, axis=0, keepdims=True))
        p = jnp.exp2(s - m_new)
        acc_ref[g] = jnp.exp2(m_prev - m_new) * acc_ref[g] + _dot(vt_ref[j], p.astype(BF16))
        m_ref[g] = m_new

    def step_lagged(j, g, s):
        m_prev = m_ref[g]
        p = jnp.exp2(s - m_prev)
        block_max = jnp.max(s, axis=0, keepdims=True)
        m_new = jnp.maximum(m_prev, block_max)
        acc_ref[g] = jnp.exp2(m_prev - m_new) * (acc_ref[g] + _dot(vt_ref[j], p.astype(BF16)))
        m_ref[g] = m_new
        jump_ref[g] = jnp.maximum(jump_ref[g], block_max - m_prev)

    def sweep(j, s, step):
        for g in range(GQA_GROUPS):
            if g + 1 < GQA_GROUPS:
                s_next = scores(j, g + 1)
            else:
                s_next = scores(jnp.minimum(j + 1, n_kv - 1), 0)
            step(j, g, s)
            s = s_next
        return s

    acc_ref[...] = jnp.zeros_like(acc_ref)
    jump_ref[...] = jnp.zeros_like(jump_ref)
    k_first = k_ref[0:FA_INIT_KEYS, :]
    for g in range(GQA_GROUPS):
        m_ref[g] = jnp.max(_dot(k_first, qt_ref[0, g * LANES:(g + 1) * LANES, :]), axis=0, keepdims=True)
    lax.fori_loop(0, n_kv, lambda j, s: sweep(j, s, step_lagged), scores(0, 0), unroll=FA_UNROLL)

    @pl.when(jnp.max(jump_ref[...]) > FA_MAX_JUMP)
    def _():
        reset()

        def body(j, carry):
            for g in range(GQA_GROUPS):
                step_exact(j, g, scores(j, g))
            return carry

        lax.fori_loop(0, n_kv, body, 0)

    for pair in range(GQA_GROUPS // 2):
        halves = []
        for g in (2 * pair, 2 * pair + 1):
            acc = acc_ref[g]
            halves.append(acc[:HEAD_DIM] / acc[HEAD_DIM:HEAD_DIM + 1])
        o_ref[:, pair * LANES:(pair + 1) * LANES] = jnp.concatenate(halves, axis=0).T.astype(o_ref.dtype)


def flash_attention(qt, k, vt):
    q_blocks = SEQ // FA_TQ
    q_per_tile = FA_TK // FA_TQ
    kv_blocks = SEQ // FA_TK
    gw = GQA_GROUPS * LANES
    return pl.pallas_call(
        _flash_kernel,
        grid=(BATCH, GQA_KV_HEADS, q_blocks),
        in_specs=[
            pl.BlockSpec((1, gw, FA_TQ), lambda b, kh, i: (b * kv_blocks + i // q_per_tile, kh, i % q_per_tile)),
            pl.BlockSpec((SEQ, LANES), lambda b, kh, i: (b, kh)),
            pl.BlockSpec((kv_blocks, FA_VROWS, FA_TK), lambda b, kh, i: (b, kh, 0)),
        ],
        out_specs=pl.BlockSpec((FA_TQ, GQA_GROUPS * HEAD_DIM), lambda b, kh, i: (b * q_blocks + i, kh)),
        out_shape=jax.ShapeDtypeStruct((TOKENS, D_MODEL), BF16),
        scratch_shapes=[
            pltpu.VMEM((GQA_GROUPS, 1, FA_TQ), F32),
            pltpu.VMEM((GQA_GROUPS, FA_VROWS, FA_TQ), F32),
            pltpu.VMEM((GQA_GROUPS, 1, FA_TQ), F32),
        ],
        compiler_params=pltpu.CompilerParams(
            dimension_semantics=("parallel", "parallel", "parallel"), vmem_limit_bytes=VMEM_LIMIT),
        name="gqa_flash",
    )(qt, k, vt)


def _router_kernel(h_ref, gin_ref, wr_ref, xn_ref, gates_ref, rank_ref, rankt_ref, count_ref):
    tm = h_ref.shape[0]
    lane = lax.broadcasted_iota(jnp.int32, (1, LANES), 1)
    x = _rms(h_ref[...], gin_ref[...])
    x_hi = x.astype(BF16)
    xn_ref[...] = x_hi
    x_lo = (x - x_hi.astype(F32)).astype(BF16)
    w = wr_ref[...]
    w_hi = w.astype(BF16)
    w_lo = (w - w_hi.astype(F32)).astype(BF16)
    logits = _dot(x_hi, w_hi) + _dot(x_lo, w_hi) + _dot(x_hi, w_lo)
    logits = jnp.where(lane < N_EXPERTS, logits, -jnp.inf)
    m1 = jnp.max(logits, axis=-1, keepdims=True)
    i1 = jnp.min(jnp.where(logits == m1, lane, LANES), axis=-1, keepdims=True)
    rest = jnp.where(lane == i1, -jnp.inf, logits)
    m2 = jnp.max(rest, axis=-1, keepdims=True)
    i2 = jnp.min(jnp.where(rest == m2, lane, LANES), axis=-1, keepdims=True)
    e2 = jnp.exp(m2 - m1)
    denom = 1.0 + e2
    gates_ref[...] = jnp.where(lane == i1, 1.0 / denom, 0.0) + jnp.where(lane == i2, e2 / denom, 0.0)
    chosen = jnp.logical_or(lane == i1, lane == i2)
    earlier = lax.broadcasted_iota(jnp.int32, (tm, tm), 1) < lax.broadcasted_iota(jnp.int32, (tm, tm), 0)
    prefix = _dot(jnp.where(earlier, 1.0, 0.0).astype(BF16), jnp.where(chosen, 1.0, 0.0).astype(BF16))
    rank = jnp.where(chosen, prefix, -1.0)
    rank_ref[...] = rank
    rankt_ref[0] = rank.T
    count_ref[0] = jnp.sum(jnp.where(chosen, 1.0, 0.0), axis=0, keepdims=True)


def _moe_kernel(cnt_ref, xn_ref, gates_ref, rank_ref, rankt_ref, wg_ref, wu_ref, wd_ref, h_ref, gout_ref, o_ref,
                xs_ref, yacc_ref, acc_ref):
    i = pl.program_id(0)
    e = pl.program_id(1)
    j = pl.program_id(2)
    last_e = pl.num_programs(1) - 1
    last_j = pl.num_programs(2) - 1
    n_rows = cnt_ref[i * N_EXPERTS + e]

    @pl.when(jnp.logical_and(e == 0, j == 0))
    def _():
        acc_ref[...] = jnp.zeros_like(acc_ref)

    def run(group, n_groups, ffn_rows=None):
        ffn_rows = ffn_rows or group

        def over_groups(fn):
            if isinstance(n_groups, int):
                for c in range(n_groups):
                    fn(c * group)
            else:
                def body(c, carry):
                    fn(pl.multiple_of(c * group, LANES))
                    return carry

                lax.fori_loop(0, n_groups, body, 0)

        @pl.when(j == 0)
        def _():
            rank_row = rankt_ref[0, pl.ds(e, 1), :]

            def compact(r0):
                slot = (lax.broadcasted_iota(jnp.int32, (group, 1), 0) + r0).astype(F32)
                onehot = jnp.where(rank_row == slot, 1.0, 0.0).astype(BF16)
                xs_ref[pl.ds(r0, group), :] = _dot(onehot, xn_ref[...]).astype(BF16)
                yacc_ref[pl.ds(r0, group), :] = jnp.zeros((group, yacc_ref.shape[1]), F32)

            over_groups(compact)

        def expert(r0):
            xs = xs_ref[pl.ds(r0, ffn_rows), :]
            gate = _dot(xs, wg_ref[0])
            up = _dot(xs, wu_ref[0])
            act = (gate * jax.nn.sigmoid(gate) * up).astype(BF16)
            yacc_ref[pl.ds(r0, ffn_rows), :] += _dot(act, wd_ref[0])

        over_groups(expert)

        @pl.when(j == last_j)
        def _():
            lane = lax.broadcasted_iota(jnp.int32, (1, LANES), 1)
            rank_col = jnp.sum(jnp.where(lane == e, rank_ref[...], 0.0), axis=-1, keepdims=True)
            gate_col = jnp.sum(jnp.where(lane == e, gates_ref[...], 0.0), axis=-1, keepdims=True)

            def expand(r0):
                slot = (lax.broadcasted_iota(jnp.int32, (1, group), 1) + r0).astype(F32)
                onehot = jnp.where(rank_col == slot, 1.0, 0.0).astype(BF16)
                acc_ref[...] += gate_col * _dot(onehot, yacc_ref[pl.ds(r0, group), :].astype(BF16))

            over_groups(expand)

    @pl.when(jnp.logical_and(n_rows > 0, n_rows <= MOE_GROUP_SMALL))
    def _():
        run(MOE_GROUP_SMALL, 1)

    lower = MOE_GROUP_SMALL
    for upper in MOE_FFN_ROWS:
        @pl.when(jnp.logical_and(n_rows > lower, n_rows <= upper))
        def _(upper=upper):
            run(MOE_GROUP, 1, upper)
        lower = upper

    @pl.when(n_rows > MOE_GROUP)
    def _():
        run(MOE_GROUP, (n_rows + (MOE_GROUP - 1)) // MOE_GROUP)

    @pl.when(jnp.logical_and(e == last_e, j == last_j))
    def _():
        o_ref[...] = h_ref[...] + _rms(acc_ref[...], gout_ref[...])


def moe_block(h, g_in, w_router, wg, wu, wd, g_out, tm=MOE_TM, tf=MOE_TF):
    t, d = h.shape
    n_e, _, f = wg.shape
    n_tiles = t // tm
    wr = jnp.pad(w_router, ((0, 0), (0, LANES - n_e)))
    xn, gates, rank, rankt, counts = pl.pallas_call(
        _router_kernel,
        grid=(n_tiles,),
        in_specs=[
            pl.BlockSpec((tm, d), lambda i: (i, 0)),
            pl.BlockSpec((1, d), lambda i: (0, 0)),
            pl.BlockSpec((d, LANES), lambda i: (0, 0)),
        ],
        out_specs=[
            pl.BlockSpec((tm, d), lambda i: (i, 0)),
            pl.BlockSpec((tm, LANES), lambda i: (i, 0)),
            pl.BlockSpec((tm, LANES), lambda i: (i, 0)),
            pl.BlockSpec((1, LANES, tm), lambda i: (i, 0, 0)),
            pl.BlockSpec((1, 1, LANES), lambda i: (i, 0, 0)),
        ],
        out_shape=[
            jax.ShapeDtypeStruct((t, d), BF16),
            jax.ShapeDtypeStruct((t, LANES), F32),
            jax.ShapeDtypeStruct((t, LANES), F32),
            jax.ShapeDtypeStruct((n_tiles, LANES, tm), F32),
            jax.ShapeDtypeStruct((n_tiles, 1, LANES), F32),
        ],
        compiler_params=pltpu.CompilerParams(
            dimension_semantics=("parallel",), vmem_limit_bytes=VMEM_LIMIT),
        name="moe_router",
    )(h, g_in.reshape(1, d), wr)
    n_rows = counts[:, 0, :n_e].astype(jnp.int32)
    max_rows = -(-tm // MOE_GROUP) * MOE_GROUP
    grid_spec = pltpu.PrefetchScalarGridSpec(
        num_scalar_prefetch=1,
        grid=(n_tiles, n_e, f // tf),
        in_specs=[
            pl.BlockSpec((tm, d), lambda i, e, j, ng: (i, 0)),
            pl.BlockSpec((tm, LANES), lambda i, e, j, ng: (i, 0)),
            pl.BlockSpec((tm, LANES), lambda i, e, j, ng: (i, 0)),
            pl.BlockSpec((1, LANES, tm), lambda i, e, j, ng: (i, 0, 0)),
            pl.BlockSpec((1, d, tf), lambda i, e, j, ng: (e, 0, j)),
            pl.BlockSpec((1, d, tf), lambda i, e, j, ng: (e, 0, j)),
            pl.BlockSpec((1, tf, d), lambda i, e, j, ng: (e, j, 0)),
            pl.BlockSpec((tm, d), lambda i, e, j, ng: (i, 0)),
            pl.BlockSpec((1, d), lambda i, e, j, ng: (0, 0)),
        ],
        out_specs=pl.BlockSpec((tm, d), lambda i, e, j, ng: (i, 0)),
        scratch_shapes=[
            pltpu.VMEM((max_rows, d), BF16),
            pltpu.VMEM((max_rows, d), F32),
            pltpu.VMEM((tm, d), F32),
        ],
    )
    return pl.pallas_call(
        _moe_kernel,
        grid_spec=grid_spec,
        out_shape=jax.ShapeDtypeStruct((t, d), F32),
        compiler_params=pltpu.CompilerParams(
            dimension_semantics=("parallel", "arbitrary", "arbitrary"), vmem_limit_bytes=VMEM_LIMIT),
        name="moe_swiglu",
    )(n_rows.reshape(-1), xn, gates, rank, rankt, wg, wu, wd, h, g_out.reshape(1, d))


def kernel(x, norm_g, na_w_qkv, na_rpb, na_w_o, gqa_w_qkv, gqa_q_norm, gqa_k_norm, gqa_w_o,
           ffn_w_gate, ffn_w_up, ffn_w_down, moe_w_router, moe_w_gate, moe_w_up, moe_w_down):
    assert x.shape == (BATCH, SEQ, D_MODEL)
    h = x.reshape(TOKENS, D_MODEL)

    log2e = float(np.log2(np.e))
    qt0, k0, vt0 = na_qkv(h, norm_g[0, 0], na_w_qkv[0].astype(BF16), (HEAD_DIM ** -0.5) * log2e)
    a = neighborhood_attention(qt0, k0, vt0, _na_bias_tables(na_rpb[0] * log2e))
    h = proj_norm_res(a, na_w_o[0].astype(BF16), norm_g[0, 1], h)
    h = ffn_block(h, norm_g[0, 2], ffn_w_gate[0].astype(BF16), ffn_w_up[0].astype(BF16),
                  ffn_w_down[0].astype(BF16), norm_g[0, 3])

    nq = GQA_Q_HEADS * HEAD_DIM
    nkv = GQA_KV_HEADS * HEAD_DIM
    wq, wk, wv = gqa_w_qkv[0][:, :nq], gqa_w_qkv[0][:, nq:nq + nkv], gqa_w_qkv[0][:, nq + nkv:]
    qt, k, vt = gqa_qkv(h, norm_g[1, 0], wq, wk, wv, gqa_q_norm[0], gqa_k_norm[0])
    o = flash_attention(qt, k, vt)
    h = proj_norm_res(o, gqa_w_o[0].astype(BF16), norm_g[1, 1], h)
    h = moe_block(h, norm_g[1, 2], moe_w_router[0], moe_w_gate[0].astype(BF16), moe_w_up[0].astype(BF16),
                  moe_w_down[0].astype(BF16), norm_g[1, 3])
    return h.reshape(BATCH, SEQ, D_MODEL)
```

```python
import functools

import jax
import jax.numpy as jnp
import numpy as np
from jax import lax
from jax.experimental import pallas as pl
from jax.experimental.pallas import tpu as pltpu

F32 = jnp.float32
BF16 = jnp.bfloat16

D_MODEL = 1024
BATCH = 2
SEQ = 8192
TOKENS = BATCH * SEQ
GRID_W = 64
GRID_H = SEQ // GRID_W
NA_HEADS = 16
HEAD_DIM = 64
WIN_H = 8
WIN_W = 16
GQA_Q_HEADS = 16
GQA_KV_HEADS = 4
GQA_GROUPS = GQA_Q_HEADS // GQA_KV_HEADS
ROPE_THETA = 10000.0
D_FF = 3584
N_EXPERTS = 8
EPS = 1e-6
LANES = 128
NEG_BIG = -1e30

VMEM_LIMIT = 56 * 1024 * 1024

NA_QROWS = 4
NA_QTOK = NA_QROWS * GRID_W
NA_BAND_BLOCKS = 3
NA_KTOK = NA_BAND_BLOCKS * NA_QTOK
NA_ROW_BLOCKS = GRID_H // NA_QROWS
NA_HEADS_PER_STEP = 16

FA_TQ = 512
FA_TK = 512
MOE_TM = 1024
MOE_TF = 896
MOE_GROUP = 384
MOE_GROUP_SMALL = 256
MOE_FFN_ROWS = (288, 320, MOE_GROUP)

FA_UNROLL = 16
FA_INIT_KEYS = 16
FA_MAX_JUMP = 24.0
FA_VROWS = 80


def _rms(x, g):
    ms = jnp.mean(x * x, axis=-1, keepdims=True)
    return x * lax.rsqrt(ms + EPS) * g


def _dot(a, b):
    return jnp.dot(a, b, preferred_element_type=F32)


def _dot_nt(a, b):
    return lax.dot_general(a, b, (((1,), (1,)), ((), ())), preferred_element_type=F32)


def _na_qkv_kernel(x_ref, g_ref, w_ref, qt_ref, k_ref, vt_ref, *, q_scale, n_chunk):
    xn = _rms(x_ref[...], g_ref[...]).astype(BF16)
    tm = xn.shape[0]
    d = k_ref.shape[1]

    def store_transposed(dst_ref, y, col0):
        for t0 in range(0, tm, NA_QTOK):
            for c0 in range(0, n_chunk, LANES):
                dst_ref[t0 // NA_QTOK, col0 + c0:col0 + c0 + LANES, :] = (
                    y[t0:t0 + NA_QTOK, c0:c0 + LANES].T.astype(BF16))

    for n0 in range(0, d, n_chunk):
        store_transposed(qt_ref, _dot(xn, w_ref[:, n0:n0 + n_chunk]) * q_scale, n0)
        k_ref[:, n0:n0 + n_chunk] = _dot(xn, w_ref[:, d + n0:d + n0 + n_chunk]).astype(BF16)
        store_transposed(vt_ref, _dot(xn, w_ref[:, 2 * d + n0:2 * d + n0 + n_chunk]), n0)


def na_qkv(x, g, w, q_scale, tm=512, n_chunk=512):
    t, d = x.shape
    n = w.shape[1]
    tiles = tm // NA_QTOK
    transposed = jax.ShapeDtypeStruct((t // NA_QTOK, d, NA_QTOK), BF16)
    return pl.pallas_call(
        functools.partial(_na_qkv_kernel, q_scale=q_scale, n_chunk=n_chunk),
        grid=(t // tm,),
        in_specs=[
            pl.BlockSpec((tm, d), lambda i: (i, 0)),
            pl.BlockSpec((1, d), lambda i: (0, 0)),
            pl.BlockSpec((d, n), lambda i: (0, 0)),
        ],
        out_specs=[
            pl.BlockSpec((tiles, d, NA_QTOK), lambda i: (i, 0, 0)),
            pl.BlockSpec((tm, d), lambda i: (i, 0)),
            pl.BlockSpec((tiles, d, NA_QTOK), lambda i: (i, 0, 0)),
        ],
        out_shape=[transposed, jax.ShapeDtypeStruct((t, d), BF16), transposed],
        compiler_params=pltpu.CompilerParams(
            dimension_semantics=("parallel",), vmem_limit_bytes=VMEM_LIMIT),
        name="na_qkv",
    )(x, g.reshape(1, d), w)


def _proj_norm_res_kernel(a_ref, w_ref, g_ref, h_ref, o_ref):
    y = _dot(a_ref[...], w_ref[...])
    o_ref[...] = h_ref[...] + _rms(y, g_ref[...])


def proj_norm_res(a, w, g, h, tm=1024):
    t, k = a.shape
    d = w.shape[1]
    return pl.pallas_call(
        _proj_norm_res_kernel,
        grid=(t // tm,),
        in_specs=[
            pl.BlockSpec((tm, k), lambda i: (i, 0)),
            pl.BlockSpec((k, d), lambda i: (0, 0)),
            pl.BlockSpec((1, d), lambda i: (0, 0)),
            pl.BlockSpec((tm, d), lambda i: (i, 0)),
        ],
        out_specs=pl.BlockSpec((tm, d), lambda i: (i, 0)),
        out_shape=jax.ShapeDtypeStruct((t, d), F32),
        compiler_params=pltpu.CompilerParams(
            dimension_semantics=("parallel",), vmem_limit_bytes=VMEM_LIMIT),
        name="proj_norm_res",
    )(a, w, g.reshape(1, d), h)


def _na_bias_expand_kernel(plane_ref, cols_ref, o_ref):
    case = pl.program_id(0)
    band_rows = NA_BAND_BLOCKS * NA_QROWS
    for i in range(NA_QROWS):
        for m in range(band_rows):
            u = plane_ref[(case * NA_QROWS + i) * band_rows + m]
            o_ref[0, 0, m * GRID_W:(m + 1) * GRID_W, i * GRID_W:(i + 1) * GRID_W] = cols_ref[0, u]


def _na_bias_tables(rpb):
    i = np.arange(NA_QROWS)[:, None]
    m = np.arange(NA_BAND_BLOCKS * NA_QROWS)[None, :]
    ridx, rvalid = [], []
    for rb in (0, 1, NA_ROW_BLOCKS - 1):
        band0 = NA_QROWS * min(max(rb - 1, 0), NA_ROW_BLOCKS - NA_BAND_BLOCKS)
        r = NA_QROWS * rb + i
        kr = band0 + m
        rs = np.clip(r - WIN_H // 2, 0, GRID_H - WIN_H)
        rvalid.append((kr >= rs) & (kr < rs + WIN_H))
        ridx.append(np.clip(kr - r + WIN_H - 1, 0, 2 * WIN_H - 2))
    n_dr = 2 * WIN_H - 1
    plane = np.where(np.stack(rvalid), np.stack(ridx), n_dr)
    c = np.arange(GRID_W)[:, None]
    kc = np.arange(GRID_W)[None, :]
    cs = np.clip(c - WIN_W // 2, 0, GRID_W - WIN_W)
    cvalid = (kc >= cs) & (kc < cs + WIN_W)
    valid_t = cvalid.T
    pick = ((kc.T - c.T + WIN_W - 1)[None] == np.arange(2 * WIN_W - 1)[:, None, None]) & valid_t[None]
    cols_t = jnp.einsum('hud,dkc->hukc', rpb, jnp.asarray(pick, F32), precision=lax.Precision.HIGHEST)
    cols_t = jnp.where(valid_t[None, None], cols_t, NEG_BIG)
    cols_t = jnp.concatenate([cols_t, jnp.full_like(cols_t[:, :1], NEG_BIG)], axis=1)
    n_cases = plane.shape[0]
    grid_spec = pltpu.PrefetchScalarGridSpec(
        num_scalar_prefetch=1,
        grid=(n_cases, NA_HEADS),
        in_specs=[pl.BlockSpec((1, n_dr + 1, GRID_W, GRID_W), lambda z, h, pr: (h, 0, 0, 0))],
        out_specs=pl.BlockSpec((1, 1, NA_KTOK, NA_QTOK), lambda z, h, pr: (z, h // 2, 0, h % 2)),
    )
    return pl.pallas_call(
        _na_bias_expand_kernel,
        grid_spec=grid_spec,
        out_shape=jax.ShapeDtypeStruct((n_cases, NA_HEADS // 2, NA_KTOK, 2 * NA_QTOK), F32),
        compiler_params=pltpu.CompilerParams(dimension_semantics=("parallel", "parallel")),
        name="na_bias_expand",
    )(jnp.asarray(plane.reshape(-1), jnp.int32), cols_t)


def _na_kernel(qt_ref, k0_ref, k1_ref, k2_ref, vt0_ref, vt1_ref, vt2_ref, bias_ref, o_ref):
    row = lax.broadcasted_iota(jnp.int32, (LANES, 1), 0)
    low = row < HEAD_DIM

    n_pairs = NA_HEADS_PER_STEP // 2

    def scores(pair):
        rows = slice(pair * LANES, (pair + 1) * LANES)
        qt = qt_ref[0, rows, :]
        k = jnp.concatenate([k0_ref[:, rows], k1_ref[:, rows], k2_ref[:, rows]], axis=0)
        zero = jnp.zeros_like(qt)
        qt2 = jnp.concatenate([jnp.where(low, qt, zero), jnp.where(low, zero, qt)], axis=1)
        return _dot(k, qt2)

    s = scores(0)
    for pair in range(n_pairs):
        s_next = scores(pair + 1) if pair + 1 < n_pairs else None
        rows = slice(pair * LANES, (pair + 1) * LANES)
        vt = jnp.concatenate([vt0_ref[0, rows, :], vt1_ref[0, rows, :], vt2_ref[0, rows, :]], axis=1)
        one = jnp.ones_like(vt)
        s = s + bias_ref[0, pair]
        p = jnp.exp2(s - jnp.max(s, axis=0, keepdims=True)).astype(BF16)
        pv_a = _dot(jnp.where(low, vt, one), p[:, :NA_QTOK])
        pv_b = _dot(jnp.where(low, one, vt), p[:, NA_QTOK:])
        both = jnp.concatenate([pv_a[:HEAD_DIM] / pv_a[HEAD_DIM:], pv_b[HEAD_DIM:] / pv_b[:HEAD_DIM]], axis=0)
        o_ref[:, rows] = both.T.astype(o_ref.dtype)
        s = s_next


def neighborhood_attention(qt, k, vt, bias_tab):
    hw = NA_HEADS_PER_STEP * HEAD_DIM
    nh = D_MODEL // hw
    blocks_per_seq = SEQ // NA_QTOK

    def band(rb):
        return jnp.clip(rb - 1, 0, NA_ROW_BLOCKS - NA_BAND_BLOCKS)

    def case(rb):
        return jnp.where(rb == 0, 0, jnp.where(rb == NA_ROW_BLOCKS - 1, 2, 1))

    qt_spec = pl.BlockSpec((1, hw, NA_QTOK), lambda h, b, rb: (b * blocks_per_seq + rb, h, 0))
    k_specs = [pl.BlockSpec((NA_QTOK, hw), lambda h, b, rb, j=j: (b * blocks_per_seq + band(rb) + j, h))
               for j in range(NA_BAND_BLOCKS)]
    vt_specs = [pl.BlockSpec((1, hw, NA_QTOK), lambda h, b, rb, j=j: (b * blocks_per_seq + band(rb) + j, h, 0))
                for j in range(NA_BAND_BLOCKS)]
    bias_spec = pl.BlockSpec((1, NA_HEADS_PER_STEP // 2, NA_KTOK, 2 * NA_QTOK),
                             lambda h, b, rb: (case(rb), h, 0, 0))
    return pl.pallas_call(
        _na_kernel,
        grid=(nh, BATCH, NA_ROW_BLOCKS),
        in_specs=[qt_spec] + k_specs + vt_specs + [bias_spec],
        out_specs=pl.BlockSpec((NA_QTOK, hw), lambda h, b, rb: (b * blocks_per_seq + rb, h)),
        out_shape=jax.ShapeDtypeStruct((TOKENS, D_MODEL), BF16),
        compiler_params=pltpu.CompilerParams(
            dimension_semantics=("parallel", "parallel", "parallel"), vmem_limit_bytes=VMEM_LIMIT),
        name="na_attention",
    )(qt, k, k, k, vt, vt, vt, bias_tab)


def _ffn_kernel(h_ref, gin_ref, wg_ref, wu_ref, wd_ref, gout_ref, o_ref, xn_ref, acc_ref):
    j = pl.program_id(1)

    @pl.when(j == 0)
    def _():
        xn_ref[...] = _rms(h_ref[...], gin_ref[...]).astype(BF16)
        acc_ref[...] = jnp.zeros_like(acc_ref)

    xn = xn_ref[...]
    gate = _dot(xn, wg_ref[...])
    up = _dot(xn, wu_ref[...])
    act = (gate * jax.nn.sigmoid(gate) * up).astype(BF16)
    acc_ref[...] += _dot(act, wd_ref[...])

    @pl.when(j == pl.num_programs(1) - 1)
    def _():
        o_ref[...] = h_ref[...] + _rms(acc_ref[...], gout_ref[...])


def ffn_block(h, g_in, wg, wu, wd, g_out, tm=1024, tf=512):
    t, d = h.shape
    f = wg.shape[1]
    return pl.pallas_call(
        _ffn_kernel,
        grid=(t // tm, f // tf),
        in_specs=[
            pl.BlockSpec((tm, d), lambda i, j: (i, 0)),
            pl.BlockSpec((1, d), lambda i, j: (0, 0)),
            pl.BlockSpec((d, tf), lambda i, j: (0, j)),
            pl.BlockSpec((d, tf), lambda i, j: (0, j)),
            pl.BlockSpec((tf, d), lambda i, j: (j, 0)),
            pl.BlockSpec((1, d), lambda i, j: (0, 0)),
        ],
        out_specs=pl.BlockSpec((tm, d), lambda i, j: (i, 0)),
        out_shape=jax.ShapeDtypeStruct((t, d), F32),
        scratch_shapes=[pltpu.VMEM((tm, d), BF16), pltpu.VMEM((tm, d), F32)],
        compiler_params=pltpu.CompilerParams(
            dimension_semantics=("parallel", "arbitrary"), vmem_limit_bytes=VMEM_LIMIT),
        name="ffn_swiglu",
    )(h, g_in.reshape(1, d), wg, wu, wd, g_out.reshape(1, d))


def _pad_heads(w, n_heads, width):
    d = w.shape[0]
    w = w.reshape(d, n_heads, HEAD_DIM)
    return jnp.pad(w, ((0, 0), (0, 0), (0, width - HEAD_DIM))).reshape(d, n_heads * width)


def _rope_tables():
    f32 = np.float32
    quarter = HEAD_DIM // 4
    freqs = (f32(ROPE_THETA) ** (-np.arange(quarter, dtype=f32) / f32(quarter))).astype(f32)
    t = np.arange(SEQ)
    row_ang = (t // GRID_W).astype(f32)[:, None] * freqs[None, :]
    col_ang = (t % GRID_W).astype(f32)[:, None] * freqs[None, :]
    zeros = np.zeros((SEQ, LANES - HEAD_DIM), f32)
    cos = np.concatenate([np.cos(row_ang), np.cos(row_ang), np.cos(col_ang), np.cos(col_ang), zeros], axis=1)
    sin = np.concatenate([-np.sin(row_ang), np.sin(row_ang), -np.sin(col_ang), np.sin(col_ang), zeros], axis=1)
    return cos.astype(f32), sin.astype(f32)


def _gqa_qkv_kernel(x_ref, g_ref, wq_ref, wk_ref, wv_ref, qg_ref, kg_ref, cos_ref, sin_ref, cost_ref, sint_ref,
                    qt_ref, k_ref, vt_ref):
    xn = _rms(x_ref[...], g_ref[...]).astype(BF16)
    tm = xn.shape[0]
    quarter = HEAD_DIM // 4

    cost = cost_ref[...]
    sint = sint_ref[...]
    qg = qg_ref[...]
    zero_rows = jnp.zeros((LANES - HEAD_DIM, tm), BF16)
    wide = 2 * LANES
    for pair in range(GQA_Q_HEADS // 2):
        if pair % 2 == 0:
            y_wide = _dot(xn, wq_ref[:, pair * LANES:pair * LANES + wide])
        yt = y_wide[:, (pair % 2) * LANES:(pair % 2 + 1) * LANES].T
        for half in range(2):
            y = yt[half * HEAD_DIM:(half + 1) * HEAD_DIM]
            ms = jnp.sum(y * y, axis=0, keepdims=True) * (1.0 / HEAD_DIM)
            y = y * lax.rsqrt(ms + EPS) * qg
            partner = jnp.concatenate(
                [y[quarter:2 * quarter], y[:quarter], y[3 * quarter:], y[2 * quarter:3 * quarter]], axis=0)
            base = (2 * pair + half) * LANES
            qt_ref[0, base:base + HEAD_DIM, :] = (y * cost + partner * sint).astype(BF16)
            qt_ref[0, base + HEAD_DIM:base + LANES, :] = zero_rows

    lane = lax.broadcasted_iota(jnp.int32, (1, LANES), 1)
    first = (lane % (2 * quarter)) < quarter
    cos = cos_ref[...]
    sin = sin_ref[...]
    for hh in range(GQA_KV_HEADS):
        if hh % 2 == 0:
            y_wide = _dot(xn, wk_ref[:, hh * LANES:hh * LANES + wide])
        y = y_wide[:, (hh % 2) * LANES:(hh % 2 + 1) * LANES]
        ms = jnp.sum(y * y, axis=-1, keepdims=True) * (1.0 / HEAD_DIM)
        y = y * lax.rsqrt(ms + EPS) * kg_ref[...]
        partner = jnp.where(first, pltpu.roll(y, LANES - quarter, 1), pltpu.roll(y, quarter, 1))
        k_ref[:, hh * LANES:(hh + 1) * LANES] = (y * cos + partner * sin).astype(BF16)

    fill = FA_VROWS - HEAD_DIM
    ones_then_zeros = jnp.where(lax.broadcasted_iota(jnp.int32, (fill, tm), 0) == 0, 1.0, 0.0).astype(BF16)
    assert GQA_KV_HEADS * HEAD_DIM == wide
    v_wide = _dot(xn, wv_ref[...])
    for pair in range(GQA_KV_HEADS // 2):
        vt = v_wide[:, pair * LANES:(pair + 1) * LANES].T
        for half in range(2):
            base = (2 * pair + half) * FA_VROWS
            vt_ref[0, base:base + HEAD_DIM, :] = vt[half * HEAD_DIM:(half + 1) * HEAD_DIM].astype(BF16)
            vt_ref[0, base + HEAD_DIM:base + FA_VROWS, :] = ones_then_zeros


def gqa_qkv(h, g, wq, wk, wv, q_gain, k_gain):
    t, d = h.shape
    tm = FA_TK
    seq_blocks = SEQ // tm
    wqb = wq.astype(BF16)
    wkp = _pad_heads(wk, GQA_KV_HEADS, LANES).astype(BF16)
    wvb = wv.astype(BF16)
    cos, sin = _rope_tables()
    q_scale = (HEAD_DIM ** -0.5) * float(np.log2(np.e))
    cost = np.ascontiguousarray((cos[:, :HEAD_DIM] * np.float32(q_scale)).T)
    sint = np.ascontiguousarray((sin[:, :HEAD_DIM] * np.float32(q_scale)).T)
    qg = q_gain.reshape(HEAD_DIM, 1)
    kg = jnp.pad(k_gain, (0, LANES - HEAD_DIM)).reshape(1, LANES)
    const = lambda shape: pl.BlockSpec(shape, lambda i: (0,) * len(shape))
    return pl.pallas_call(
        _gqa_qkv_kernel,
        grid=(t // tm,),
        in_specs=[
            pl.BlockSpec((tm, d), lambda i: (i, 0)),
            const((1, d)),
            const(wqb.shape),
            const(wkp.shape),
            const(wvb.shape),
            const((HEAD_DIM, 1)),
            const((1, LANES)),
            pl.BlockSpec((tm, LANES), lambda i: (i % seq_blocks, 0)),
            pl.BlockSpec((tm, LANES), lambda i: (i % seq_blocks, 0)),
            pl.BlockSpec((HEAD_DIM, tm), lambda i: (0, i % seq_blocks)),
            pl.BlockSpec((HEAD_DIM, tm), lambda i: (0, i % seq_blocks)),
        ],
        out_specs=[
            pl.BlockSpec((1, GQA_Q_HEADS * LANES, tm), lambda i: (i, 0, 0)),
            pl.BlockSpec((tm, GQA_KV_HEADS * LANES), lambda i: (i, 0)),
            pl.BlockSpec((1, GQA_KV_HEADS * FA_VROWS, tm), lambda i: (i, 0, 0)),
        ],
        out_shape=[
            jax.ShapeDtypeStruct((t // tm, GQA_Q_HEADS * LANES, tm), BF16),
            jax.ShapeDtypeStruct((t, GQA_KV_HEADS * LANES), BF16),
            jax.ShapeDtypeStruct((t // tm, GQA_KV_HEADS * FA_VROWS, tm), BF16),
        ],
        compiler_params=pltpu.CompilerParams(
            dimension_semantics=("parallel",), vmem_limit_bytes=VMEM_LIMIT),
        name="gqa_qkv",
    )(h, g.reshape(1, d), wqb, wkp, wvb, qg, kg, cos, sin, cost, sint)


def _flash_kernel(qt_ref, k_ref, vt_ref, o_ref, m_ref, acc_ref, jump_ref):
    n_kv = SEQ // FA_TK

    def scores(j, g):
        kj = k_ref[pl.ds(pl.multiple_of(j * FA_TK, FA_TK), FA_TK), :]
        return _dot(kj, qt_ref[0, g * LANES:(g + 1) * LANES, :])

    def reset():
        m_ref[...] = jnp.full(m_ref.shape, -jnp.inf, F32)
        acc_ref[...] = jnp.zeros_like(acc_ref)

    def step_exact(j, g, s):
        m_prev = m_ref[g]
        m_new = jnp.maximum(m_prev, jnp.max(s, axis=0, keepdims=True))
        p = jnp.exp2(s - m_new)
        acc_ref[g] = jnp.exp2(m_prev - m_new) * acc_ref[g] + _dot(vt_ref[j], p.astype(BF16))
        m_ref[g] = m_new

    def step_lagged(j, g, s):
        m_prev = m_ref[g]
        p = jnp.exp2(s - m_prev)
        block_max = jnp.max(s, axis=0, keepdims=True)
        m_new = jnp.maximum(m_prev, block_max)
        acc_ref[g] = jnp.exp2(m_prev - m_new) * (acc_ref[g] + _dot(vt_ref[j], p.astype(BF16)))
        m_ref[g] = m_new
        jump_ref[g] = jnp.maximum(jump_ref[g], block_max - m_prev)

    def sweep(j, s, step):
        for g in range(GQA_GROUPS):
            if g + 1 < GQA_GROUPS:
                s_next = scores(j, g + 1)
            else:
                s_next = scores(jnp.minimum(j + 1, n_kv - 1), 0)
            step(j, g, s)
            s = s_next
        return s

    acc_ref[...] = jnp.zeros_like(acc_ref)
    jump_ref[...] = jnp.zeros_like(jump_ref)
    k_first = k_ref[0:FA_INIT_KEYS, :]
    for g in range(GQA_GROUPS):
        m_ref[g] = jnp.max(_dot(k_first, qt_ref[0, g * LANES:(g + 1) * LANES, :]), axis=0, keepdims=True)
    lax.fori_loop(0, n_kv, lambda j, s: sweep(j, s, step_lagged), scores(0, 0), unroll=FA_UNROLL)

    @pl.when(jnp.max(jump_ref[...]) > FA_MAX_JUMP)
    def _():
        reset()

        def body(j, carry):
            for g in range(GQA_GROUPS):
                step_exact(j, g, scores(j, g))
            return carry

        lax.fori_loop(0, n_kv, body, 0)

    for pair in range(GQA_GROUPS // 2):
        halves = []
        for g in (2 * pair, 2 * pair + 1):
            acc = acc_ref[g]
            halves.append(acc[:HEAD_DIM] / acc[HEAD_DIM:HEAD_DIM + 1])
        o_ref[:, pair * LANES:(pair + 1) * LANES] = jnp.concatenate(halves, axis=0).T.astype(o_ref.dtype)


def flash_attention(qt, k, vt):
    q_blocks = SEQ // FA_TQ
    q_per_tile = FA_TK // FA_TQ
    kv_blocks = SEQ // FA_TK
    gw = GQA_GROUPS * LANES
    return pl.pallas_call(
        _flash_kernel,
        grid=(BATCH, GQA_KV_HEADS, q_blocks),
        in_specs=[
            pl.BlockSpec((1, gw, FA_TQ), lambda b, kh, i: (b * kv_blocks + i // q_per_tile, kh, i % q_per_tile)),
            pl.BlockSpec((SEQ, LANES), lambda b, kh, i: (b, kh)),
            pl.BlockSpec((kv_blocks, FA_VROWS, FA_TK), lambda b, kh, i: (b, kh, 0)),
        ],
        out_specs=pl.BlockSpec((FA_TQ, GQA_GROUPS * HEAD_DIM), lambda b, kh, i: (b * q_blocks + i, kh)),
        out_shape=jax.ShapeDtypeStruct((TOKENS, D_MODEL), BF16),
        scratch_shapes=[
            pltpu.VMEM((GQA_GROUPS, 1, FA_TQ), F32),
            pltpu.VMEM((GQA_GROUPS, FA_VROWS, FA_TQ), F32),
            pltpu.VMEM((GQA_GROUPS, 1, FA_TQ), F32),
        ],
        compiler_params=pltpu.CompilerParams(
            dimension_semantics=("parallel", "parallel", "parallel"), vmem_limit_bytes=VMEM_LIMIT),
        name="gqa_flash",
    )(qt, k, vt)


def _router_kernel(h_ref, gin_ref, wr_ref, xn_ref, gates_ref, rank_ref, rankt_ref, count_ref):
    tm = h_ref.shape[0]
    lane = lax.broadcasted_iota(jnp.int32, (1, LANES), 1)
    x = _rms(h_ref[...], gin_ref[...])
    x_hi = x.astype(BF16)
    xn_ref[...] = x_hi
    x_lo = (x - x_hi.astype(F32)).astype(BF16)
    w = wr_ref[...]
    w_hi = w.astype(BF16)
    w_lo = (w - w_hi.astype(F32)).astype(BF16)
    logits = _dot(x_hi, w_hi) + _dot(x_lo, w_hi) + _dot(x_hi, w_lo)
    logits = jnp.where(lane < N_EXPERTS, logits, -jnp.inf)
    m1 = jnp.max(logits, axis=-1, keepdims=True)
    i1 = jnp.min(jnp.where(logits == m1, lane, LANES), axis=-1, keepdims=True)
    rest = jnp.where(lane == i1, -jnp.inf, logits)
    m2 = jnp.max(rest, axis=-1, keepdims=True)
    i2 = jnp.min(jnp.where(rest == m2, lane, LANES), axis=-1, keepdims=True)
    e2 = jnp.exp(m2 - m1)
    denom = 1.0 + e2
    gates_ref[...] = jnp.where(lane == i1, 1.0 / denom, 0.0) + jnp.where(lane == i2, e2 / denom, 0.0)
    chosen = jnp.logical_or(lane == i1, lane == i2)
    earlier = lax.broadcasted_iota(jnp.int32, (tm, tm), 1) < lax.broadcasted_iota(jnp.int32, (tm, tm), 0)
    prefix = _dot(jnp.where(earlier, 1.0, 0.0).astype(BF16), jnp.where(chosen, 1.0, 0.0).astype(BF16))
    rank = jnp.where(chosen, prefix, -1.0)
    rank_ref[...] = rank
    rankt_ref[0] = rank.T
    count_ref[0] = jnp.sum(jnp.where(chosen, 1.0, 0.0), axis=0, keepdims=True)


def _moe_kernel(cnt_ref, xn_ref, gates_ref, rank_ref, rankt_ref, wg_ref, wu_ref, wd_ref, h_ref, gout_ref, o_ref,
                xs_ref, yacc_ref, acc_ref):
    i = pl.program_id(0)
    e = pl.program_id(1)
    j = pl.program_id(2)
    last_e = pl.num_programs(1) - 1
    last_j = pl.num_programs(2) - 1
    n_rows = cnt_ref[i * N_EXPERTS + e]

    @pl.when(jnp.logical_and(e == 0, j == 0))
    def _():
        acc_ref[...] = jnp.zeros_like(acc_ref)

    def run(group, n_groups, ffn_rows=None):
        ffn_rows = ffn_rows or group

        def over_groups(fn):
            if isinstance(n_groups, int):
                for c in range(n_groups):
                    fn(c * group)
            else:
                def body(c, carry):
                    fn(pl.multiple_of(c * group, LANES))
                    return carry

                lax.fori_loop(0, n_groups, body, 0)

        @pl.when(j == 0)
        def _():
            rank_row = rankt_ref[0, pl.ds(e, 1), :]

            def compact(r0):
                slot = (lax.broadcasted_iota(jnp.int32, (group, 1), 0) + r0).astype(F32)
                onehot = jnp.where(rank_row == slot, 1.0, 0.0).astype(BF16)
                xs_ref[pl.ds(r0, group), :] = _dot(onehot, xn_ref[...]).astype(BF16)
                yacc_ref[pl.ds(r0, group), :] = jnp.zeros((group, yacc_ref.shape[1]), F32)

            over_groups(compact)

        def expert(r0):
            xs = xs_ref[pl.ds(r0, ffn_rows), :]
            gate = _dot(xs, wg_ref[0])
            up = _dot(xs, wu_ref[0])
            act = (gate * jax.nn.sigmoid(gate) * up).astype(BF16)
            yacc_ref[pl.ds(r0, ffn_rows), :] += _dot(act, wd_ref[0])

        over_groups(expert)

        @pl.when(j == last_j)
        def _():
            lane = lax.broadcasted_iota(jnp.int32, (1, LANES), 1)
            rank_col = jnp.sum(jnp.where(lane == e, rank_ref[...], 0.0), axis=-1, keepdims=True)
            gate_col = jnp.sum(jnp.where(lane == e, gates_ref[...], 0.0), axis=-1, keepdims=True)

            def expand(r0):
                slot = (lax.broadcasted_iota(jnp.int32, (1, group), 1) + r0).astype(F32)
                onehot = jnp.where(rank_col == slot, 1.0, 0.0).astype(BF16)
                acc_ref[...] += gate_col * _dot(onehot, yacc_ref[pl.ds(r0, group), :].astype(BF16))

            over_groups(expand)

    @pl.when(jnp.logical_and(n_rows > 0, n_rows <= MOE_GROUP_SMALL))
    def _():
        run(MOE_GROUP_SMALL, 1)

    lower = MOE_GROUP_SMALL
    for upper in MOE_FFN_ROWS:
        @pl.when(jnp.logical_and(n_rows > lower, n_rows <= upper))
        def _(upper=upper):
            run(MOE_GROUP, 1, upper)
        lower = upper

    @pl.when(n_rows > MOE_GROUP)
    def _():
        run(MOE_GROUP, (n_rows + (MOE_GROUP - 1)) // MOE_GROUP)

    @pl.when(jnp.logical_and(e == last_e, j == last_j))
    def _():
        o_ref[...] = h_ref[...] + _rms(acc_ref[...], gout_ref[...])


def moe_block(h, g_in, w_router, wg, wu, wd, g_out, tm=MOE_TM, tf=MOE_TF):
    t, d = h.shape
    n_e, _, f = wg.shape
    n_tiles = t // tm
    wr = jnp.pad(w_router, ((0, 0), (0, LANES - n_e)))
    xn, gates, rank, rankt, counts = pl.pallas_call(
        _router_kernel,
        grid=(n_tiles,),
        in_specs=[
            pl.BlockSpec((tm, d), lambda i: (i, 0)),
            pl.BlockSpec((1, d), lambda i: (0, 0)),
            pl.BlockSpec((d, LANES), lambda i: (0, 0)),
        ],
        out_specs=[
            pl.BlockSpec((tm, d), lambda i: (i, 0)),
            pl.BlockSpec((tm, LANES), lambda i: (i, 0)),
            pl.BlockSpec((tm, LANES), lambda i: (i, 0)),
            pl.BlockSpec((1, LANES, tm), lambda i: (i, 0, 0)),
            pl.BlockSpec((1, 1, LANES), lambda i: (i, 0, 0)),
        ],
        out_shape=[
            jax.ShapeDtypeStruct((t, d), BF16),
            jax.ShapeDtypeStruct((t, LANES), F32),
            jax.ShapeDtypeStruct((t, LANES), F32),
            jax.ShapeDtypeStruct((n_tiles, LANES, tm), F32),
            jax.ShapeDtypeStruct((n_tiles, 1, LANES), F32),
        ],
        compiler_params=pltpu.CompilerParams(
            dimension_semantics=("parallel",), vmem_limit_bytes=VMEM_LIMIT),
        name="moe_router",
    )(h, g_in.reshape(1, d), wr)
    n_rows = counts[:, 0, :n_e].astype(jnp.int32)
    max_rows = -(-tm // MOE_GROUP) * MOE_GROUP
    grid_spec = pltpu.PrefetchScalarGridSpec(
        num_scalar_prefetch=1,
        grid=(n_tiles, n_e, f // tf),
        in_specs=[
            pl.BlockSpec((tm, d), lambda i, e, j, ng: (i, 0)),
            pl.BlockSpec((tm, LANES), lambda i, e, j, ng: (i, 0)),
            pl.BlockSpec((tm, LANES), lambda i, e, j, ng: (i, 0)),
            pl.BlockSpec((1, LANES, tm), lambda i, e, j, ng: (i, 0, 0)),
            pl.BlockSpec((1, d, tf), lambda i, e, j, ng: (e, 0, j)),
            pl.BlockSpec((1, d, tf), lambda i, e, j, ng: (e, 0, j)),
            pl.BlockSpec((1, tf, d), lambda i, e, j, ng: (e, j, 0)),
            pl.BlockSpec((tm, d), lambda i, e, j, ng: (i, 0)),
            pl.BlockSpec((1, d), lambda i, e, j, ng: (0, 0)),
        ],
        out_specs=pl.BlockSpec((tm, d), lambda i, e, j, ng: (i, 0)),
        scratch_shapes=[
            pltpu.VMEM((max_rows, d), BF16),
            pltpu.VMEM((max_rows, d), F32),
            pltpu.VMEM((tm, d), F32),
        ],
    )
    return pl.pallas_call(
        _moe_kernel,
        grid_spec=grid_spec,
        out_shape=jax.ShapeDtypeStruct((t, d), F32),
        compiler_params=pltpu.CompilerParams(
            dimension_semantics=("parallel", "arbitrary", "arbitrary"), vmem_limit_bytes=VMEM_LIMIT),
        name="moe_swiglu",
    )(n_rows.reshape(-1), xn, gates, rank, rankt, wg, wu, wd, h, g_out.reshape(1, d))


def kernel(x, norm_g, na_w_qkv, na_rpb, na_w_o, gqa_w_qkv, gqa_q_norm, gqa_k_norm, gqa_w_o,
           ffn_w_gate, ffn_w_up, ffn_w_down, moe_w_router, moe_w_gate, moe_w_up, moe_w_down):
    assert x.shape == (BATCH, SEQ, D_MODEL)
    h = x.reshape(TOKENS, D_MODEL)

    log2e = float(np.log2(np.e))
    qt0, k0, vt0 = na_qkv(h, norm_g[0, 0], na_w_qkv[0].astype(BF16), (HEAD_DIM ** -0.5) * log2e)
    a = neighborhood_attention(qt0, k0, vt0, _na_bias_tables(na_rpb[0] * log2e))
    h = proj_norm_res(a, na_w_o[0].astype(BF16), norm_g[0, 1], h)
    h = ffn_block(h, norm_g[0, 2], ffn_w_gate[0].astype(BF16), ffn_w_up[0].astype(BF16),
                  ffn_w_down[0].astype(BF16), norm_g[0, 3])

    nq = GQA_Q_HEADS * HEAD_DIM
    nkv = GQA_KV_HEADS * HEAD_DIM
    wq, wk, wv = gqa_w_qkv[0][:, :nq], gqa_w_qkv[0][:, nq:nq + nkv], gqa_w_qkv[0][:, nq + nkv:]
    qt, k, vt = gqa_qkv(h, norm_g[1, 0], wq, wk, wv, gqa_q_norm[0], gqa_k_norm[0])
    o = flash_attention(qt, k, vt)
    h = proj_norm_res(o, gqa_w_o[0].astype(BF16), norm_g[1, 1], h)
    h = moe_block(h, norm_g[1, 2], moe_w_router[0], moe_w_gate[0].astype(BF16), moe_w_up[0].astype(BF16),
                  moe_w_down[0].astype(BF16), norm_g[1, 3])
    return h.reshape(BATCH, SEQ, D_MODEL)
```
